```python
import math
import jax, jax.numpy as jnp
from jax import lax
import numpy as np

D_MODEL = 1024
BATCH = 16
SEQ = 4096
DEPTH = 4

N_MIXERS = 3
N_MLSTM_LAYERS = (DEPTH + 2) // 3
N_SSM_LAYERS = (DEPTH + 1) // 3
N_RWKV_LAYERS = DEPTH // 3
DEEPNORM_ALPHA = (2 * DEPTH) ** 0.25
DEEPNORM_BETA = (8 * DEPTH) ** -0.25
LN_EPS = 1e-5
RMS_EPS = 1e-6

ML_HEADS = 4
ML_DV = D_MODEL // ML_HEADS
ML_DQK = ML_DV // 2
ML_CHUNK = 64
ML_GATE_CAP = 15.0
ML_IN = 2 * ML_HEADS * ML_DQK + 2 * ML_HEADS * ML_DV + 2 * ML_HEADS

SSM_DINNER = 2 * D_MODEL
SSM_HEADDIM = 64
SSM_HEADS = SSM_DINNER // SSM_HEADDIM
SSM_STATE = 128
SSM_GROUPS = 4
SSM_CONV = 4
SSM_CHUNK = 128
SSM_CONV_CH = SSM_DINNER + 2 * SSM_GROUPS * SSM_STATE
SSM_IN = SSM_DINNER + SSM_CONV_CH + SSM_HEADS

RW_HEADDIM = 64
RW_HEADS = D_MODEL // RW_HEADDIM
RW_DECAY_LORA = 64
RW_AAA_LORA = 64
RW_GATE_LORA = 128
RW_LNX_EPS = 64e-5
RW_N_MIX = 6

FFN_HIDDEN = int(math.ceil(8 * D_MODEL / 3 / 256)) * 256

kernel_name = 'hybrid_mlstm_mamba2_rwkv7_deepnorm'


def layer_norm(x, g, b):
    xf = x.astype(jnp.float32)
    mu = jnp.mean(xf, axis=-1, keepdims=True)
    var = jnp.mean(jnp.square(xf - mu), axis=-1, keepdims=True)
    return ((xf - mu) * lax.rsqrt(var + LN_EPS) * g + b).astype(x.dtype)


def soft_cap(t):
    return ML_GATE_CAP * jnp.tanh(t / ML_GATE_CAP)


def mlstm_mixer(x, w_in, b_gate, norm_w, w_out):
    Bsz, S, _ = x.shape
    H, DK, DV, L = ML_HEADS, ML_DQK, ML_DV, ML_CHUNK
    NC = S // L
    proj = (x @ w_in).astype(jnp.float32)
    q, k, v, o, gates = jnp.split(
        proj, [H * DK, 2 * H * DK, 2 * H * DK + H * DV, 2 * H * DK + 2 * H * DV], axis=-1)
    gates = soft_cap(gates + b_gate.astype(jnp.float32))
    i_pre = gates[..., :H]
    log_f = jax.nn.log_sigmoid(gates[..., H:])

    def chunks(t, d):
        return t.reshape(Bsz, NC, L, H, d).transpose(1, 0, 3, 2, 4)

    def gchunks(t):
        return t.reshape(Bsz, NC, L, H).transpose(1, 0, 3, 2)

    causal = jnp.tril(jnp.ones((L, L), dtype=bool))

    def step(carry, inp):
        C, n, m = carry
        q_, k_, v_, i_, lf = inp
        b = jnp.cumsum(lf, axis=-1)
        dmat = jnp.where(causal, b[..., :, None] - b[..., None, :] + i_[..., None, :], -jnp.inf)
        inter = b + m[..., None]
        m_t = jnp.maximum(inter, jnp.max(dmat, axis=-1))
        wts = jnp.exp(dmat - m_t[..., None])
        sc = jnp.exp(inter - m_t)
        qk = jnp.einsum('bhtd,bhsd->bhts', q_, k_) * wts
        num = jnp.einsum('bhts,bhsv->bhtv', qk, v_) + sc[..., None] * jnp.einsum('bhtd,bhdv->bhtv', q_, C)
        den = jnp.sum(qk, axis=-1) + sc * jnp.einsum('bhtd,bhd->bht', q_, n)
        h = num / jnp.maximum(jnp.abs(den), jnp.exp(-m_t))[..., None]
        b_last = b[..., -1]
        g_s = b_last[..., None] - b + i_
        m_new = jnp.maximum(b_last + m, jnp.max(g_s, axis=-1))
        ws = jnp.exp(g_s - m_new[..., None])
        dec = jnp.exp(b_last + m - m_new)
        C_new = dec[..., None, None] * C + jnp.einsum('bhs,bhsd,bhsv->bhdv', ws, k_, v_)
        n_new = dec[..., None] * n + jnp.einsum('bhs,bhsd->bhd', ws, k_)
        return (C_new, n_new, m_new), h

    init = (jnp.zeros((Bsz, H, DK, DV), jnp.float32),
            jnp.zeros((Bsz, H, DK), jnp.float32),
            jnp.zeros((Bsz, H), jnp.float32))
    xs = (chunks(q, DK) * (DK ** -0.5), chunks(k, DK), chunks(v, DV), gchunks(i_pre), gchunks(log_f))
    _, h = lax.scan(step, init, xs)
    h = h.transpose(1, 0, 3, 2, 4).reshape(Bsz, S, H, DV)
    h = h * lax.rsqrt(jnp.mean(h * h, axis=-1, keepdims=True) + RMS_EPS)
    h = h.reshape(Bsz, S, H * DV) * norm_w * jax.nn.sigmoid(o)
    return h.astype(x.dtype) @ w_out


def causal_depthwise_conv(u, w, b):
    K, C = w.shape
    out = lax.conv_general_dilated(
        u, w[:, None, :].astype(u.dtype), window_strides=(1,), padding=[(K - 1, 0)],
        dimension_numbers=('NWC', 'WIO', 'NWC'), feature_group_count=C)
    return out + b


def ssd_chunk_scan(xs, dt, A, Bm, Cm):
    Bsz, S, _ = xs.shape
    H, P, G, N, L = SSM_HEADS, SSM_HEADDIM, SSM_GROUPS, SSM_STATE, SSM_CHUNK
    HG = H // G
    NC = S // L
    xc = xs.reshape(Bsz, NC, L, G, HG, P).transpose(1, 0, 3, 4, 2, 5)
    dtc = dt.reshape(Bsz, NC, L, G, HG).transpose(1, 0, 3, 4, 2)
    Bc = Bm.reshape(Bsz, NC, L, G, N).transpose(1, 0, 3, 2, 4)
    Cc = Cm.reshape(Bsz, NC, L, G, N).transpose(1, 0, 3, 2, 4)
    A_g = A.reshape(G, HG)[None, :, :, None]
    causal = jnp.tril(jnp.ones((L, L), dtype=bool))

    def step(state, inp):
        x_, dt_, B_, C_ = inp
        cum = jnp.cumsum(dt_ * A_g, axis=-1)
        seg = jnp.where(causal, cum[..., :, None] - cum[..., None, :], -jnp.inf)
        cb = jnp.einsum('bgtn,bgsn->bgts', C_, B_)
        mmat = cb[:, :, None] * jnp.exp(seg) * dt_[..., None, :]
        y = jnp.einsum('bghts,bghsp->bghtp', mmat, x_)
        y = y + jnp.einsum('bgtn,bghpn->bghtp', C_, state) * jnp.exp(cum)[..., None]
        w_s = jnp.exp(cum[..., -1:] - cum) * dt_
        state = (state * jnp.exp(cum[..., -1])[..., None, None]
                 + jnp.einsum('bghs,bgsn,bghsp->bghpn', w_s, B_, x_))
        return state, y

    state0 = jnp.zeros((Bsz, G, HG, P, N), jnp.float32)
    _, y = lax.scan(step, state0, (xc, dtc, Bc, Cc))
    return y.transpose(1, 0, 4, 2, 3, 5).reshape(Bsz, S, H * P)


def mamba2_mixer(x, w_in, conv_w, conv_b, dt_bias, a_log, d_skip, norm_w, w_out):
    Bsz, S, _ = x.shape
    DI, H, P, G = SSM_DINNER, SSM_HEADS, SSM_HEADDIM, SSM_GROUPS
    f32 = jnp.float32
    proj = x @ w_in
    z, xbc, dt = jnp.split(proj, [DI, DI + SSM_CONV_CH], axis=-1)
    xbc = jax.nn.silu(causal_depthwise_conv(xbc, conv_w, conv_b)).astype(f32)
    xs, Bm, Cm = jnp.split(xbc, [DI, DI + G * SSM_STATE], axis=-1)
    dt = jax.nn.softplus(dt.astype(f32) + dt_bias.astype(f32))
    A = -jnp.exp(a_log.astype(f32))
    y = ssd_chunk_scan(xs, dt, A, Bm, Cm)
    y = y + (xs.reshape(Bsz, S, H, P) * d_skip[:, None]).reshape(Bsz, S, DI)
    y = (y * jax.nn.silu(z.astype(f32))).reshape(Bsz, S, G, DI // G)
    y = y * lax.rsqrt(jnp.mean(y * y, axis=-1, keepdims=True) + RMS_EPS)
    y = y.reshape(Bsz, S, DI) * norm_w
    return y.astype(x.dtype) @ w_out


def rwkv7_scan(r, w, k, v, a, b):
    Bsz, _, H, K = r.shape

    def step(st, inp):
        r_, w_, k_, v_, a_, b_ = inp
        sa = jnp.einsum('bhvk,bhk->bhv', st, a_)
        st = st * w_[:, :, None, :] + sa[..., None] * b_[:, :, None, :] + v_[..., None] * k_[:, :, None, :]
        return st, jnp.einsum('bhvk,bhk->bhv', st, r_)

    tm = lambda t: jnp.moveaxis(t, 1, 0)
    st0 = jnp.zeros((Bsz, H, K, K), jnp.float32)
    _, y = lax.scan(step, st0, (tm(r), tm(w), tm(k), tm(v), tm(a), tm(b)))
    return jnp.moveaxis(y, 0, 1)


def rwkv7_mixer(x, mix, w_rkv, w0, w1, w2, a0, a1, a2, g1, g2, k_k, k_a, r_k, lnx_w, lnx_b, w_out):
    Bsz, S, D = x.shape
    H, K = RW_HEADS, RW_HEADDIM
    f32 = jnp.float32
    xx = jnp.pad(x, ((0, 0), (1, 0), (0, 0)))[:, :-1] - x
    xr, xw, xk, xv, xa, xg = (x + xx * mix[j] for j in range(RW_N_MIX))
    rkv = jnp.einsum('cbsd,cde->cbse', jnp.stack([xr, xk, xv]), w_rkv).astype(f32)
    r, k, v = rkv[0], rkv[1], rkv[2]
    w = -jax.nn.softplus(-(w0 + jnp.tanh(xw @ w1) @ w2).astype(f32)) - 0.5
    a = jax.nn.sigmoid((a0 + (xa @ a1) @ a2).astype(f32))
    g = (jax.nn.sigmoid(xg @ g1) @ g2).astype(f32)
    heads = lambda t: t.reshape(Bsz, S, H, K)
    kk = heads(k * k_k)
    kk = kk * lax.rsqrt(jnp.maximum(jnp.sum(kk * kk, axis=-1, keepdims=True), 1e-24))
    k = heads(k * (1.0 + (a - 1.0) * k_a))
    r_h, v_h = heads(r), heads(v)
    y = rwkv7_scan(r_h, heads(jnp.exp(-jnp.exp(w))), k, v_h, -kk, kk * heads(a))
    mu = jnp.mean(y, axis=-1, keepdims=True)
    var = jnp.mean(jnp.square(y - mu), axis=-1, keepdims=True)
    y = ((y - mu) * lax.rsqrt(var + RW_LNX_EPS)).reshape(Bsz, S, D) * lnx_w + lnx_b
    y = y + (jnp.sum(r_h * k * r_k, axis=-1, keepdims=True) * v_h).reshape(Bsz, S, D)
    return (y * g).astype(x.dtype) @ w_out


def swiglu(x, w_in, w_out):
    gate, up = jnp.split(x @ w_in, 2, axis=-1)
    return (jax.nn.silu(gate) * up) @ w_out


def setup_inputs(seed: int = 0) -> dict:
    key = jax.random.key(seed)
    ks = jax.random.split(key, 33)
    f32 = jnp.float32
    D, F = D_MODEL, FFN_HIDDEN
    nm, ns, nr = N_MLSTM_LAYERS, N_SSM_LAYERS, N_RWKV_LAYERS
    nrm = lambda k, shape, scale: jax.random.normal(k, shape, f32) * scale
    H = ML_HEADS
    gn = jax.random.normal(ks[6], (nm, 2 * H), f32)
    ml_b_gate = jnp.concatenate([0.1 * gn[:, :H], 3.0 + 0.5 * gn[:, H:]], axis=-1)
    dt0 = jnp.exp(jax.random.uniform(ks[12], (ns, SSM_HEADS), f32, math.log(1e-3), math.log(1e-1)))
    return {
        'x': nrm(ks[0], (BATCH, SEQ, D), 1.0),
        'ln_g': 1.0 + nrm(ks[1], (DEPTH, 2, D), 0.02),
        'ln_b': nrm(ks[2], (DEPTH, 2, D), 0.02),
        'ffn_w_in': nrm(ks[3], (DEPTH, D, 2 * F), D ** -0.5),
        'ffn_w_out': nrm(ks[4], (DEPTH, F, D), DEEPNORM_BETA * F ** -0.5),
        'ml_w_in': nrm(ks[5], (nm, D, ML_IN), D ** -0.5),
        'ml_b_gate': ml_b_gate,
        'ml_norm_w': 1.0 + nrm(ks[7], (nm, D), 0.02),
        'ml_w_out': nrm(ks[8], (nm, D, D), DEEPNORM_BETA * D ** -0.5),
        'ssm_w_in': nrm(ks[9], (ns, D, SSM_IN), D ** -0.5),
        'ssm_conv_w': nrm(ks[10], (ns, SSM_CONV, SSM_CONV_CH), SSM_CONV ** -0.5),
        'ssm_conv_b': nrm(ks[11], (ns, SSM_CONV_CH), 0.02),
        'ssm_dt_bias': dt0 + jnp.log(-jnp.expm1(-dt0)),
        'ssm_a_log': jnp.log(jax.random.uniform(ks[13], (ns, SSM_HEADS), f32, 1.0, 16.0)),
        'ssm_d': 1.0 + nrm(ks[14], (ns, SSM_HEADS), 0.1),
        'ssm_norm_w': 1.0 + nrm(ks[15], (ns, SSM_DINNER), 0.02),
        'ssm_w_out': nrm(ks[16], (ns, SSM_DINNER, D), DEEPNORM_BETA * SSM_DINNER ** -0.5),
        'rw_mix': jax.random.uniform(ks[17], (nr, RW_N_MIX, D), f32),
        'rw_w_rkv': nrm(ks[18], (nr, 3, D, D), D ** -0.5),
        'rw_w0': jax.random.uniform(ks[19], (nr, D), f32, -6.0, -1.0),
        'rw_w1': nrm(ks[20], (nr, D, RW_DECAY_LORA), D ** -0.5),
        'rw_w2': nrm(ks[21], (nr, RW_DECAY_LORA, D), 0.5 * RW_DECAY_LORA ** -0.5),
        'rw_a0': nrm(ks[22], (nr, D), 0.1),
        'rw_a1': nrm(ks[23], (nr, D, RW_AAA_LORA), D ** -0.5),
        'rw_a2': nrm(ks[24], (nr, RW_AAA_LORA, D), 0.5 * RW_AAA_LORA ** -0.5),
        'rw_g1': nrm(ks[25], (nr, D, RW_GATE_LORA), D ** -0.5),
        'rw_g2': nrm(ks[26], (nr, RW_GATE_LORA, D), RW_GATE_LORA ** -0.5),
        'rw_k_k': 0.85 + nrm(ks[27], (nr, D), 0.02),
        'rw_k_a': 1.0 + nrm(ks[28], (nr, D), 0.02),
        'rw_r_k': nrm(ks[29], (nr, RW_HEADS, RW_HEADDIM), 0.1),
        'rw_lnx_w': 1.0 + nrm(ks[30], (nr, D), 0.02),
        'rw_lnx_b': nrm(ks[31], (nr, D), 0.02),
        'rw_w_out': nrm(ks[32], (nr, D, D), DEEPNORM_BETA * D ** -0.5),
    }


def reference(x, ln_g, ln_b, ffn_w_in, ffn_w_out,
              ml_w_in, ml_b_gate, ml_norm_w, ml_w_out,
              ssm_w_in, ssm_conv_w, ssm_conv_b, ssm_dt_bias, ssm_a_log, ssm_d, ssm_norm_w, ssm_w_out,
              rw_mix, rw_w_rkv, rw_w0, rw_w1, rw_w2, rw_a0, rw_a1, rw_a2, rw_g1, rw_g2,
              rw_k_k, rw_k_a, rw_r_k, rw_lnx_w, rw_lnx_b, rw_w_out):
    h = x
    for i in range(DEPTH):
        kind, j = i % N_MIXERS, i // N_MIXERS
        if kind == 0:
            y = mlstm_mixer(h, ml_w_in[j], ml_b_gate[j], ml_norm_w[j], ml_w_out[j])
        elif kind == 1:
            y = mamba2_mixer(h, ssm_w_in[j], ssm_conv_w[j], ssm_conv_b[j], ssm_dt_bias[j],
                             ssm_a_log[j], ssm_d[j], ssm_norm_w[j], ssm_w_out[j])
        else:
            y = rwkv7_mixer(h, rw_mix[j], rw_w_rkv[j], rw_w0[j], rw_w1[j], rw_w2[j], rw_a0[j],
                            rw_a1[j], rw_a2[j], rw_g1[j], rw_g2[j], rw_k_k[j], rw_k_a[j],
                            rw_r_k[j], rw_lnx_w[j], rw_lnx_b[j], rw_w_out[j])
        h = layer_norm(DEEPNORM_ALPHA * h + y, ln_g[i, 0], ln_b[i, 0])
        h = layer_norm(DEEPNORM_ALPHA * h + swiglu(h, ffn_w_in[i], ffn_w_out[i]), ln_g[i, 1], ln_b[i, 1])
    return h
```

```python
import functools
import math

import jax
import jax.numpy as jnp
from jax import lax
from jax.experimental import pallas as pl
from jax.experimental.pallas import tpu as pltpu

F32 = jnp.float32
BF16 = jnp.bfloat16

D_MODEL = 1024
DEPTH = 4
N_MIXERS = 3
DEEPNORM_ALPHA = (2 * DEPTH) ** 0.25
LN_EPS = 1e-5
RMS_EPS = 1e-6

ML_HEADS = 4
ML_DV = D_MODEL // ML_HEADS
ML_DQK = ML_DV // 2
ML_GATE_CAP = 15.0
ML_QKVO = 2 * ML_HEADS * ML_DQK + 2 * ML_HEADS * ML_DV

SSM_DINNER = 2 * D_MODEL
SSM_HEADDIM = 64
SSM_HEADS = SSM_DINNER // SSM_HEADDIM
SSM_STATE = 128
SSM_GROUPS = 4
SSM_CONV = 4
SSM_HPG = SSM_HEADS // SSM_GROUPS
SSM_GW = SSM_DINNER // SSM_GROUPS
SSM_ZX = 2 * SSM_DINNER + 2 * SSM_GROUPS * SSM_STATE

RW_HEADDIM = 64
RW_HEADS = D_MODEL // RW_HEADDIM
RW_LNX_EPS = 64e-5

FFN_HIDDEN = int(math.ceil(8 * D_MODEL / 3 / 256)) * 256

LANES = 128
SUBLANES = 8
VMEM_LIMIT = 56 * 1024 * 1024

ML_CHUNK = 64
SSM_CHUNK = 128
RW_CHUNK = 64


def _params(sem):
    return pltpu.CompilerParams(dimension_semantics=sem, vmem_limit_bytes=VMEM_LIMIT)


def _dot(a, b, dims=((1,), (0,))):
    return lax.dot_general(a.astype(BF16), b.astype(BF16), (dims, ((), ())),
                           preferred_element_type=F32)


def _split3(a):
    hi = a.astype(BF16)
    r1 = a - hi.astype(F32)
    mid = r1.astype(BF16)
    lo = (r1 - mid.astype(F32)).astype(BF16)
    return hi, mid, lo


def _dot_sel(a, sel, dims=((1,), (0,))):
    sel = sel.astype(BF16)
    return sum(lax.dot_general(p, sel, (dims, ((), ())), preferred_element_type=F32)
               for p in _split3(a))


def _sel_dot(sel, a, dims=((1,), (0,))):
    sel = sel.astype(BF16)
    return sum(lax.dot_general(sel, p, (dims, ((), ())), preferred_element_type=F32)
               for p in _split3(a))


def _dot_f32(a, b, dims=((1,), (0,))):
    ah, am, al = _split3(a)
    bh, bm, bl = _split3(b)
    d = lambda x, y: lax.dot_general(x, y, (dims, ((), ())), preferred_element_type=F32)
    return (d(ah, bh) + (d(ah, bm) + d(am, bh)) +
            (d(ah, bl) + d(am, bm) + d(al, bh)))


def _softplus(x):
    return jnp.maximum(x, 0.0) + jnp.log1p(jnp.exp(-jnp.abs(x)))


def _sigmoid(x):
    return 1.0 / (1.0 + jnp.exp(-x))


def _silu(x):
    return x * _sigmoid(x)


def _layer_norm(x, g, b):
    mu = jnp.mean(x, axis=-1, keepdims=True)
    xc = x - mu
    var = jnp.mean(xc * xc, axis=-1, keepdims=True)
    return xc * lax.rsqrt(var + LN_EPS) * g + b


def _tri(n, strict=False, upper=False):
    r = lax.broadcasted_iota(jnp.int32, (n, n), 0)
    c = lax.broadcasted_iota(jnp.int32, (n, n), 1)
    if upper:
        r, c = c, r
    return (c < r) if strict else (c <= r)


def _linear_kernel(x_ref, w_ref, o_ref):
    o_ref[...] = _dot(x_ref[...], w_ref[...]).astype(o_ref.dtype)


def _linear(x, w, out_dtype, tm, tn, name):
    t, k = x.shape
    n = w.shape[1]
    assert t % tm == 0 and n % tn == 0
    return pl.pallas_call(
        _linear_kernel,
        grid=(t // tm, n // tn),
        in_specs=[pl.BlockSpec((tm, k), lambda i, j: (i, 0)),
                  pl.BlockSpec((k, tn), lambda i, j: (0, j))],
        out_specs=pl.BlockSpec((tm, tn), lambda i, j: (i, j)),
        out_shape=jax.ShapeDtypeStruct((t, n), out_dtype),
        compiler_params=_params(("parallel", "arbitrary")),
        name=name,
    )(x, w)


def _linear_f32_kernel(x_ref, w_ref, o_ref):
    o_ref[...] = _dot_f32(x_ref[...], w_ref[...])


def _linear_f32(x, w, tm, name):
    t, k = x.shape
    n = w.shape[1]
    return pl.pallas_call(
        _linear_f32_kernel,
        grid=(t // tm,),
        in_specs=[pl.BlockSpec((tm, k), lambda i: (i, 0)),
                  pl.BlockSpec((k, n), lambda i: (0, 0))],
        out_specs=pl.BlockSpec((tm, n), lambda i: (i, 0)),
        out_shape=jax.ShapeDtypeStruct((t, n), F32),
        compiler_params=_params(("parallel",)),
        name=name,
    )(x, w)


def _out_ln_kernel(y_ref, w_ref, res_ref, g_ref, b_ref, o_ref):
    acc = _dot(y_ref[...], w_ref[...])
    o_ref[...] = _layer_norm(DEEPNORM_ALPHA * res_ref[...] + acc, g_ref[...], b_ref[...])


def _out_ln(y, w, res, g, b, tm, name):
    t, k = y.shape
    d = w.shape[1]
    return pl.pallas_call(
        _out_ln_kernel,
        grid=(t // tm,),
        in_specs=[pl.BlockSpec((tm, k), lambda i: (i, 0)),
                  pl.BlockSpec((k, d), lambda i: (0, 0)),
                  pl.BlockSpec((tm, d), lambda i: (i, 0)),
                  pl.BlockSpec((1, d), lambda i: (0, 0)),
                  pl.BlockSpec((1, d), lambda i: (0, 0))],
        out_specs=pl.BlockSpec((tm, d), lambda i: (i, 0)),
        out_shape=jax.ShapeDtypeStruct((t, d), F32),
        compiler_params=_params(("parallel",)),
        name=name,
    )(y, w, res, g.reshape(1, d), b.reshape(1, d))


def _ffn_kernel(x_ref, wg_ref, wu_ref, wo_ref, g_ref, b_ref, o_ref, acc_ref):
    j = pl.program_id(1)
    x = x_ref[...]
    xb = x.astype(BF16)
    gate = _dot(xb, wg_ref[...])
    up = _dot(xb, wu_ref[...])
    part = _dot(_silu(gate) * up, wo_ref[...])

    @pl.when(j == 0)
    def _():
        acc_ref[...] = part

    @pl.when(j > 0)
    def _():
        acc_ref[...] += part

    @pl.when(j == pl.num_programs(1) - 1)
    def _():
        o_ref[...] = _layer_norm(DEEPNORM_ALPHA * x + acc_ref[...], g_ref[...], b_ref[...])


def _ffn_ln(x, w_in, w_out, g, b, tm, tf):
    t, d = x.shape
    f = w_out.shape[0]
    nf = f // tf
    return pl.pallas_call(
        _ffn_kernel,
        grid=(t // tm, nf),
        in_specs=[pl.BlockSpec((tm, d), lambda i, j: (i, 0)),
                  pl.BlockSpec((d, tf), lambda i, j: (0, j)),
                  pl.BlockSpec((d, tf), lambda i, j: (0, j + nf)),
                  pl.BlockSpec((tf, d), lambda i, j: (j, 0)),
                  pl.BlockSpec((1, d), lambda i, j: (0, 0)),
                  pl.BlockSpec((1, d), lambda i, j: (0, 0))],
        out_specs=pl.BlockSpec((tm, d), lambda i, j: (i, 0)),
        out_shape=jax.ShapeDtypeStruct((t, d), F32),
        scratch_shapes=[pltpu.VMEM((tm, d), F32)],
        compiler_params=_params(("parallel", "arbitrary")),
        name="ffn_ln",
    )(x, w_in, w_in, w_out, g.reshape(1, d), b.reshape(1, d))


def _rows_to_chunk_rows(a, bsz, nc, l):
    c = a.shape[1]
    return a.reshape(bsz, nc, l, c).transpose(0, 1, 3, 2)


def _mlstm_kernel(q_ref, k_ref, v_ref, o_ref, gc_ref, gr_ref, bc_ref, br_ref, nw_ref,
                  out_ref, c_ref, n_ref, m_ref, *, l):
    h_, dk, dv = ML_HEADS, ML_DQK, ML_DV

    @pl.when(pl.program_id(1) == 0)
    def _():
        c_ref[...] = jnp.zeros_like(c_ref)
        n_ref[...] = jnp.zeros_like(n_ref)
        m_ref[...] = jnp.zeros_like(m_ref)

    cap = lambda t: ML_GATE_CAP * jnp.tanh(t / ML_GATE_CAP)
    gc = cap(gc_ref[...] + bc_ref[...])
    gr = cap(gr_ref[...] + br_ref[...])
    i_col, lf_col = gc[:, :h_], -_softplus(-gc[:, h_:])
    i_row, lf_row = gr[:h_, :], -_softplus(-gr[h_:, :])
    causal = _tri(l)
    b_col = _sel_dot(causal, lf_col)
    b_row = _dot_sel(lf_row, _tri(l, upper=True))
    scale = dk ** -0.5

    for h in range(h_):
        q = q_ref[:, h * dk:(h + 1) * dk]
        k = k_ref[:, h * dk:(h + 1) * dk]
        v = v_ref[:, h * dv:(h + 1) * dv]
        bc, br = b_col[:, h:h + 1], b_row[h:h + 1, :]
        ic, ir = i_col[:, h:h + 1], i_row[h:h + 1, :]
        m_prev = m_ref[h][:, :1]
        c_prev = c_ref[h]
        n_prev = n_ref[h]

        dmat = jnp.where(causal, bc - br + ir, -jnp.inf)
        inter = bc + m_prev
        m_t = jnp.maximum(inter, jnp.max(dmat, axis=1, keepdims=True))
        wts = jnp.exp(dmat - m_t)
        sc = jnp.exp(inter - m_t) * scale
        qk = _dot(q, k, ((1,), (1,))) * (wts * scale)
        num = _dot(qk, v) + sc * _dot(q, c_prev)
        den = (jnp.sum(qk, axis=1, keepdims=True)
               + sc * jnp.sum(q.astype(F32) * n_prev, axis=1, keepdims=True))
        hh = num / jnp.maximum(jnp.abs(den), jnp.exp(-m_t))

        b_last = bc[l - 1:l, :]
        g_row = b_last - br + ir
        g_col = b_last - bc + ic
        m_new = jnp.maximum(b_last + m_prev, jnp.max(g_row, axis=1, keepdims=True))
        kw = k.astype(F32) * jnp.exp(g_col - m_new)
        dec = jnp.exp(b_last + m_prev - m_new)
        c_ref[h] = dec * c_prev + _dot(kw, v, ((0,), (0,)))
        n_ref[h] = dec * n_prev + jnp.sum(kw, axis=0, keepdims=True)
        m_ref[h] = jnp.broadcast_to(m_new, (1, LANES))

        hn = hh * lax.rsqrt(jnp.mean(hh * hh, axis=1, keepdims=True) + RMS_EPS)
        sl = slice(h * dv, (h + 1) * dv)
        out_ref[:, sl] = (hn * nw_ref[:, sl] * _sigmoid(o_ref[:, sl].astype(F32))
                          ).astype(out_ref.dtype)


def _mlstm_core(proj, gpre, b_gate, norm_w, bsz, seq, l):
    h_, dk, dv = ML_HEADS, ML_DQK, ML_DV
    nc = seq // l
    t = bsz * seq
    row = lambda b, c: (b * nc + c, 0)
    g_rows = _rows_to_chunk_rows(gpre, bsz, nc, l)
    return pl.pallas_call(
        functools.partial(_mlstm_kernel, l=l),
        grid=(bsz, nc),
        in_specs=[pl.BlockSpec((l, h_ * dk), lambda b, c: (b * nc + c, 0)),
                  pl.BlockSpec((l, h_ * dk), lambda b, c: (b * nc + c, 1)),
                  pl.BlockSpec((l, h_ * dv), lambda b, c: (b * nc + c, 1)),
                  pl.BlockSpec((l, h_ * dv), lambda b, c: (b * nc + c, 2)),
                  pl.BlockSpec((l, 2 * h_), row),
                  pl.BlockSpec((None, None, 2 * h_, l), lambda b, c: (b, c, 0, 0)),
                  pl.BlockSpec((1, 2 * h_), lambda b, c: (0, 0)),
                  pl.BlockSpec((2 * h_, 1), lambda b, c: (0, 0)),
                  pl.BlockSpec((1, h_ * dv), lambda b, c: (0, 0))],
        out_specs=pl.BlockSpec((l, h_ * dv), row),
        out_shape=jax.ShapeDtypeStruct((t, h_ * dv), BF16),
        scratch_shapes=[pltpu.VMEM((h_, dk, dv), F32),
                        pltpu.VMEM((h_, 1, dk), F32),
                        pltpu.VMEM((h_, 1, LANES), F32)],
        compiler_params=_params(("parallel", "arbitrary")),
        name="mlstm_core",
    )(proj, proj, proj, proj, gpre, g_rows, b_gate.reshape(1, 2 * h_),
      b_gate.reshape(2 * h_, 1), norm_w.reshape(1, h_ * dv))


def _pad_cols(w, n):
    return jnp.pad(w, ((0, 0), (0, n - w.shape[1])))


def _mlstm_layer(x, res, w_in, b_gate, norm_w, w_out, ln_g, ln_b, bsz, seq):
    proj = _linear(x, w_in[:, :ML_QKVO].astype(BF16), BF16, 512, 1024, "mlstm_in")
    gpre = _linear_f32(x, _pad_cols(w_in[:, ML_QKVO:], LANES), 512, "mlstm_gates")
    hn = _mlstm_core(proj, gpre[:, :2 * ML_HEADS], b_gate, norm_w, bsz, seq, ML_CHUNK)
    return _out_ln(hn, w_out.astype(BF16), res, ln_g, ln_b, 512, "mlstm_out_ln")


def _ssd_kernel(z_ref, x_ref, bm_ref, cm_ref, dtc_ref, dtr_ref,
                cwx_ref, cwb_ref, cwc_ref, cbx_ref, cbb_ref, cbc_ref,
                dbc_ref, dbr_ref, alc_ref, alr_ref, dsk_ref, nw_ref,
                out_ref, st_ref, px_ref, pb_ref, pc_ref, y_ref, *, l):
    hpg, p, n = SSM_HPG, SSM_HEADDIM, SSM_STATE
    tail = SUBLANES

    @pl.when(pl.program_id(2) == 0)
    def _():
        st_ref[...] = jnp.zeros_like(st_ref)
        px_ref[0:tail, :] = jnp.zeros((tail, px_ref.shape[1]), F32)
        pb_ref[0:tail, :] = jnp.zeros((tail, n), F32)
        pc_ref[0:tail, :] = jnp.zeros((tail, n), F32)

    def conv_silu(pad_ref, cur_ref, w_ref, b_ref):
        pad_ref[tail:tail + l, :] = cur_ref[...].astype(F32)
        acc = b_ref[...] + w_ref[0:1, :] * pad_ref[tail - 3:tail - 3 + l, :]
        for j in range(1, SSM_CONV):
            off = tail - (SSM_CONV - 1) + j
            acc = acc + w_ref[j:j + 1, :] * pad_ref[off:off + l, :]
        pad_ref[0:tail, :] = pad_ref[l:l + tail, :]
        return _silu(acc)

    xs = conv_silu(px_ref, x_ref, cwx_ref, cbx_ref)
    bg = conv_silu(pb_ref, bm_ref, cwb_ref, cbb_ref)
    cg = conv_silu(pc_ref, cm_ref, cwc_ref, cbc_ref)

    dt_c = _softplus(dtc_ref[...] + dbc_ref[...])
    dt_r = _softplus(dtr_ref[...] + dbr_ref[...])
    causal = _tri(l)
    cum_c = _sel_dot(causal, dt_c * -jnp.exp(alc_ref[...]))
    cum_r = _dot_sel(dt_r * -jnp.exp(alr_ref[...]), _tri(l, upper=True))
    cb = _dot(cg, bg, ((1,), (1,)))
    bgb = bg.astype(BF16)
    cgb = cg.astype(BF16)

    for j in range(hpg):
        cc, cr = cum_c[:, j:j + 1], cum_r[j:j + 1, :]
        xj = xs[:, j * p:(j + 1) * p]
        st = st_ref[j]
        mm = cb * jnp.exp(jnp.where(causal, cc - cr, -jnp.inf)) * dt_r[j:j + 1, :]
        y = _dot(mm, xj) + _dot(cgb, st, ((1,), (1,))) * jnp.exp(cc)
        c_last = cc[l - 1:l, :]
        w_s = jnp.exp(c_last - cc) * dt_c[:, j:j + 1]
        st_ref[j] = st * jnp.exp(c_last) + _dot(xj * w_s, bgb, ((0,), (0,)))
        y_ref[:, j * p:(j + 1) * p] = y + xj * dsk_ref[:, j * p:(j + 1) * p]

    y = y_ref[...] * _silu(z_ref[...].astype(F32))
    y = y * lax.rsqrt(jnp.mean(y * y, axis=1, keepdims=True) + RMS_EPS)
    out_ref[...] = (y * nw_ref[...]).astype(out_ref.dtype)


def _ssd_core(proj, dtpre, conv_w, conv_b, dt_bias, a_log, d_skip, norm_w, bsz, seq, l):
    g_, hpg, p, n, gw = SSM_GROUPS, SSM_HPG, SSM_HEADDIM, SSM_STATE, SSM_GW
    di = SSM_DINNER
    nc = seq // l
    t = bsz * seq
    dt_cols = dtpre.reshape(t, g_, hpg).transpose(1, 0, 2)
    dt_rows = dt_cols.reshape(g_, bsz, nc, l, hpg).transpose(0, 1, 2, 4, 3)
    per_head = lambda a: a.reshape(g_, 1, hpg)
    d_cols = jnp.repeat(d_skip, p).reshape(1, di)
    row = lambda b, g, c: b * nc + c
    xoff, boff, coff = di // gw, 2 * di // n, 2 * di // n + g_
    grp_vec = lambda w: pl.BlockSpec((None, 1, hpg), lambda b, g, c: (g, 0, 0))
    grp_vec_t = lambda w: pl.BlockSpec((None, hpg, 1), lambda b, g, c: (g, 0, 0))
    cwx, cwb, cwc = conv_w[:, :di], conv_w[:, di:di + g_ * n], conv_w[:, di + g_ * n:]
    cb2 = conv_b.reshape(1, -1)
    cbx, cbb, cbc = cb2[:, :di], cb2[:, di:di + g_ * n], cb2[:, di + g_ * n:]
    return pl.pallas_call(
        functools.partial(_ssd_kernel, l=l),
        grid=(bsz, g_, nc),
        in_specs=[pl.BlockSpec((l, gw), lambda b, g, c: (row(b, g, c), g)),
                  pl.BlockSpec((l, gw), lambda b, g, c: (row(b, g, c), xoff + g)),
                  pl.BlockSpec((l, n), lambda b, g, c: (row(b, g, c), boff + g)),
                  pl.BlockSpec((l, n), lambda b, g, c: (row(b, g, c), coff + g)),
                  pl.BlockSpec((None, l, hpg), lambda b, g, c: (g, row(b, g, c), 0)),
                  pl.BlockSpec((None, None, None, hpg, l), lambda b, g, c: (g, b, c, 0, 0)),
                  pl.BlockSpec((SSM_CONV, gw), lambda b, g, c: (0, g)),
                  pl.BlockSpec((SSM_CONV, n), lambda b, g, c: (0, g)),
                  pl.BlockSpec((SSM_CONV, n), lambda b, g, c: (0, g)),
                  pl.BlockSpec((1, gw), lambda b, g, c: (0, g)),
                  pl.BlockSpec((1, n), lambda b, g, c: (0, g)),
                  pl.BlockSpec((1, n), lambda b, g, c: (0, g)),
                  grp_vec(None), grp_vec_t(None), grp_vec(None), grp_vec_t(None),
                  pl.BlockSpec((1, gw), lambda b, g, c: (0, g)),
                  pl.BlockSpec((1, gw), lambda b, g, c: (0, g))],
        out_specs=pl.BlockSpec((l, gw), lambda b, g, c: (row(b, g, c), g)),
        out_shape=jax.ShapeDtypeStruct((t, di), BF16),
        scratch_shapes=[pltpu.VMEM((hpg, p, n), F32),
                        pltpu.VMEM((l + SUBLANES, gw), F32),
                        pltpu.VMEM((l + SUBLANES, n), F32),
                        pltpu.VMEM((l + SUBLANES, n), F32),
                        pltpu.VMEM((l, gw), F32)],
        compiler_params=_params(("parallel", "parallel", "arbitrary")),
        name="ssd_core",
    )(proj, proj, proj, proj, dt_cols, dt_rows, cwx, cwb, cwc, cbx, cbb, cbc,
      per_head(dt_bias), per_head(dt_bias).transpose(0, 2, 1),
      per_head(a_log), per_head(a_log).transpose(0, 2, 1),
      d_cols, norm_w.reshape(1, di))


def _mamba_layer(x, res, w_in, conv_w, conv_b, dt_bias, a_log, d_skip, norm_w, w_out,
                 ln_g, ln_b, bsz, seq):
    proj = _linear(x, w_in[:, :SSM_ZX].astype(BF16), BF16, 512, 1024, "ssm_in")
    dtpre = _linear_f32(x, _pad_cols(w_in[:, SSM_ZX:], LANES), 512, "ssm_dt")
    y = _ssd_core(proj, dtpre[:, :SSM_HEADS], conv_w, conv_b, dt_bias, a_log, d_skip,
                  norm_w, bsz, seq, SSM_CHUNK)
    return _out_ln(y, w_out.astype(BF16), res, ln_g, ln_b, 512, "ssm_out_ln")


def _rwkv_prep_kernel(x_ref, xp_ref, mix_ref, wrkv_ref, w0_ref, w1_ref, w2_ref,
                      a0_ref, a1_ref, a2_ref, g1_ref, g2_ref,
                      r_ref, k_ref, v_ref, lw_ref, a_ref, g_ref):
    x = x_ref[...]
    xx = xp_ref[...] - x
    mixed = lambda j: x + xx * mix_ref[j:j + 1, :]
    r_ref[...] = _dot(mixed(0), wrkv_ref[0])
    k_ref[...] = _dot(mixed(2), wrkv_ref[1])
    v_ref[...] = _dot(mixed(3), wrkv_ref[2])
    w_raw = w0_ref[...] + _dot(jnp.tanh(_dot(mixed(1), w1_ref[...])), w2_ref[...])
    w = -_softplus(-w_raw) - 0.5
    lw_ref[...] = -jnp.exp(w)
    a_ref[...] = _sigmoid(a0_ref[...] + _dot(_dot(mixed(4), a1_ref[...]), a2_ref[...]))
    g_ref[...] = _dot(_sigmoid(_dot(mixed(5), g1_ref[...])), g2_ref[...])


def _rwkv_prep(x, xprev, mix, w_rkv, w0, w1, w2, a0, a1, a2, g1, g2, tm):
    t, d = x.shape
    full = lambda a: pl.BlockSpec(a.shape, lambda i: (0,) * a.ndim)
    rowblk = pl.BlockSpec((tm, d), lambda i: (i, 0))
    ws = [mix, w_rkv.astype(BF16), w0.reshape(1, d), w1.astype(BF16), w2.astype(BF16),
          a0.reshape(1, d), a1.astype(BF16), a2.astype(BF16), g1.astype(BF16),
          g2.astype(BF16)]
    return pl.pallas_call(
        _rwkv_prep_kernel,
        grid=(t // tm,),
        in_specs=[rowblk, rowblk] + [full(a) for a in ws],
        out_specs=[rowblk] * 6,
        out_shape=[jax.ShapeDtypeStruct((t, d), F32)] * 6,
        compiler_params=_params(("parallel",)),
        name="rwkv_prep",
    )(x, xprev, *ws)


def _rwkv_scan_kernel(r_ref, k_ref, v_ref, lw_ref, a_ref, g_ref,
                      kk_ref, ka_ref, rk_ref, lnw_ref, lnb_ref,
                      out_ref, s_ref, *, l):
    hd = RW_HEADDIM
    nh = r_ref.shape[1] // hd

    @pl.when(pl.program_id(2) == 0)
    def _():
        s_ref[...] = jnp.zeros_like(s_ref)

    incl = _tri(l)
    strict = _tri(l, strict=True)
    lw = lw_ref[...]
    gcum = _sel_dot(incl, lw)
    p_incl = jnp.exp(gcum)
    p_excl = jnp.exp(gcum - lw)
    p_inv = jnp.exp(-gcum)
    p_tail = jnp.exp(gcum[l - 1:l, :] - gcum)
    p_last = p_incl[l - 1:l, :]
    a_sig = a_ref[...]
    k_in = k_ref[...]
    kmod = k_in * (1.0 + (a_sig - 1.0) * ka_ref[...])
    kk_raw = k_in * kk_ref[...]
    r_all = r_ref[...]
    v_all = v_ref[...]
    rk_all = r_all * kmod * rk_ref[...]

    for h in range(nh):
        sl = slice(h * hd, (h + 1) * hd)
        kk = kk_raw[:, sl]
        kk = kk * lax.rsqrt(jnp.maximum(jnp.sum(kk * kk, axis=1, keepdims=True), 1e-24))
        a_vec = -kk
        b_vec = kk * a_sig[:, sl]
        r, k, v = r_all[:, sl], kmod[:, sl], v_all[:, sl]
        s0 = s_ref[h]

        a_dec = a_vec * p_excl[:, sl]
        r_dec = r * p_incl[:, sl]
        b_inv = b_vec * p_inv[:, sl]
        k_inv = k * p_inv[:, sl]
        nt = ((1,), (1,))
        a_ab = jnp.where(strict, _dot(a_dec, b_inv, nt), 0.0)
        a_ak = jnp.where(strict, _dot(a_dec, k_inv, nt), 0.0)
        m_rb = jnp.where(incl, _dot(r_dec, b_inv, nt), 0.0)
        m_rk = jnp.where(incl, _dot(r_dec, k_inv, nt), 0.0)

        u = _dot(a_dec, s0, nt) + _dot(a_ak, v)
        npow = a_ab
        steps = l.bit_length() - 1
        for j in range(steps):
            u = u + _dot_f32(npow, u)
            if j + 1 < steps:
                npow = _dot_f32(npow, npow)

        y = _dot(r_dec, s0, nt) + _dot(m_rb, u) + _dot(m_rk, v)
        tn = ((0,), (0,))
        s_ref[h] = (s0 * p_last[:, sl] + _dot(u, b_vec * p_tail[:, sl], tn)
                    + _dot(v, k * p_tail[:, sl], tn))

        mu = jnp.mean(y, axis=1, keepdims=True)
        yc = y - mu
        var = jnp.mean(yc * yc, axis=1, keepdims=True)
        yn = yc * lax.rsqrt(var + RW_LNX_EPS) * lnw_ref[:, sl] + lnb_ref[:, sl]
        yn = yn + jnp.sum(rk_all[:, sl], axis=1, keepdims=True) * v
        out_ref[:, sl] = (yn * g_ref[:, sl]).astype(out_ref.dtype)


def _rwkv_scan(r, k, v, lw, a, g, k_k, k_a, r_k, lnx_w, lnx_b, bsz, seq, l):
    t, d = r.shape
    nc = seq // l
    hb = LANES // RW_HEADDIM
    blk = pl.BlockSpec((l, LANES), lambda b, h, c: (b * nc + c, h))
    vec = pl.BlockSpec((1, LANES), lambda b, h, c: (0, h))
    vecs = [a_.reshape(1, d) for a_ in (k_k, k_a, r_k, lnx_w, lnx_b)]
    return pl.pallas_call(
        functools.partial(_rwkv_scan_kernel, l=l),
        grid=(bsz, d // LANES, nc),
        in_specs=[blk] * 6 + [vec] * 5,
        out_specs=blk,
        out_shape=jax.ShapeDtypeStruct((t, d), BF16),
        scratch_shapes=[pltpu.VMEM((hb, RW_HEADDIM, RW_HEADDIM), F32)],
        compiler_params=_params(("parallel", "parallel", "arbitrary")),
        name="rwkv_scan",
    )(r, k, v, lw, a, g, *vecs)


def _rwkv_layer(x, res, mix, w_rkv, w0, w1, w2, a0, a1, a2, g1, g2, k_k, k_a, r_k,
                lnx_w, lnx_b, w_out, ln_g, ln_b, bsz, seq):
    d = x.shape[1]
    x3 = x.reshape(bsz, seq, d)
    xprev = jnp.pad(x3, ((0, 0), (1, 0), (0, 0)))[:, :-1].reshape(bsz * seq, d)
    r, k, v, lw, a, g = _rwkv_prep(x, xprev, mix, w_rkv, w0, w1, w2, a0, a1, a2, g1, g2, 256)
    y = _rwkv_scan(r, k, v, lw, a, g, k_k, k_a, r_k, lnx_w, lnx_b, bsz, seq, RW_CHUNK)
    return _out_ln(y, w_out.astype(BF16), res, ln_g, ln_b, 512, "rwkv_out_ln")


def kernel(x, ln_g, ln_b, ffn_w_in, ffn_w_out, ml_w_in, ml_b_gate, ml_norm_w, ml_w_out, ssm_w_in, ssm_conv_w, ssm_conv_b, ssm_dt_bias, ssm_a_log, ssm_d, ssm_norm_w, ssm_w_out, rw_mix, rw_w_rkv, rw_w0, rw_w1, rw_w2, rw_a0, rw_a1, rw_a2, rw_g1, rw_g2, rw_k_k, rw_k_a, rw_r_k, rw_lnx_w, rw_lnx_b, rw_w_out):
    bsz, seq, d = x.shape
    h = x.reshape(bsz * seq, d)
    for i in range(DEPTH):
        kind, j = i % N_MIXERS, i // N_MIXERS
        if kind == 0:
            h = _mlstm_layer(h, h, ml_w_in[j], ml_b_gate[j], ml_norm_w[j], ml_w_out[j],
                             ln_g[i, 0], ln_b[i, 0], bsz, seq)
        elif kind == 1:
            h = _mamba_layer(h, h, ssm_w_in[j], ssm_conv_w[j], ssm_conv_b[j], ssm_dt_bias[j],
                             ssm_a_log[j], ssm_d[j], ssm_norm_w[j], ssm_w_out[j],
                             ln_g[i, 0], ln_b[i, 0], bsz, seq)
        else:
            h = _rwkv_layer(h, h, rw_mix[j], rw_w_rkv[j], rw_w0[j], rw_w1[j], rw_w2[j],
                            rw_a0[j], rw_a1[j], rw_a2[j], rw_g1[j], rw_g2[j], rw_k_k[j],
                            rw_k_a[j], rw_r_k[j].reshape(-1), rw_lnx_w[j], rw_lnx_b[j],
                            rw_w_out[j], ln_g[i, 0], ln_b[i, 0], bsz, seq)
        h = _ffn_ln(h, ffn_w_in[i].astype(BF16), ffn_w_out[i].astype(BF16),
                    ln_g[i, 1], ln_b[i, 1], 512, 1408)
    return h.reshape(bsz, seq, d)
```

```python
import functools
import math

import jax
import jax.numpy as jnp
from jax import lax
from jax.experimental import pallas as pl
from jax.experimental.pallas import tpu as pltpu

F32 = jnp.float32
BF16 = jnp.bfloat16

D_MODEL = 1024
DEPTH = 4
N_MIXERS = 3
DEEPNORM_ALPHA = (2 * DEPTH) ** 0.25
LN_EPS = 1e-5
RMS_EPS = 1e-6

ML_HEADS = 4
ML_DV = D_MODEL // ML_HEADS
ML_DQK = ML_DV // 2
ML_GATE_CAP = 15.0
ML_QKVO = 2 * ML_HEADS * ML_DQK + 2 * ML_HEADS * ML_DV

SSM_DINNER = 2 * D_MODEL
SSM_HEADDIM = 64
SSM_HEADS = SSM_DINNER // SSM_HEADDIM
SSM_STATE = 128
SSM_GROUPS = 4
SSM_CONV = 4
SSM_HPG = SSM_HEADS // SSM_GROUPS
SSM_GW = SSM_DINNER // SSM_GROUPS
SSM_ZX = 2 * SSM_DINNER + 2 * SSM_GROUPS * SSM_STATE

RW_HEADDIM = 64
RW_HEADS = D_MODEL // RW_HEADDIM
RW_LNX_EPS = 64e-5

FFN_HIDDEN = int(math.ceil(8 * D_MODEL / 3 / 256)) * 256

LANES = 128
SUBLANES = 8
VMEM_LIMIT = 56 * 1024 * 1024

ML_CHUNK = 64
SSM_CHUNK = 128
RW_CHUNK = 64


def _params(sem):
    return pltpu.CompilerParams(dimension_semantics=sem, vmem_limit_bytes=VMEM_LIMIT)


def _dot(a, b, dims=((1,), (0,))):
    return lax.dot_general(a.astype(BF16), b.astype(BF16), (dims, ((), ())),
                           preferred_element_type=F32)


def _split3(a):
    hi = a.astype(BF16)
    r1 = a - hi.astype(F32)
    mid = r1.astype(BF16)
    lo = (r1 - mid.astype(F32)).astype(BF16)
    return hi, mid, lo


def _dot_sel(a, sel, dims=((1,), (0,))):
    sel = sel.astype(BF16)
    return sum(lax.dot_general(p, sel, (dims, ((), ())), preferred_element_type=F32)
               for p in _split3(a))


def _sel_dot(sel, a, dims=((1,), (0,))):
    sel = sel.astype(BF16)
    return sum(lax.dot_general(sel, p, (dims, ((), ())), preferred_element_type=F32)
               for p in _split3(a))


def _dot_f32(a, b, dims=((1,), (0,))):
    ah, am, al = _split3(a)
    bh, bm, bl = _split3(b)
    d = lambda x, y: lax.dot_general(x, y, (dims, ((), ())), preferred_element_type=F32)
    return (d(ah, bh) + (d(ah, bm) + d(am, bh)) +
            (d(ah, bl) + d(am, bm) + d(al, bh)))


def _softplus(x):
    return jnp.maximum(x, 0.0) + jnp.log1p(jnp.exp(-jnp.abs(x)))


def _sigmoid(x):
    return 1.0 / (1.0 + jnp.exp(-x))


def _silu(x):
    return x * _sigmoid(x)


def _layer_norm(x, g, b):
    mu = jnp.mean(x, axis=-1, keepdims=True)
    xc = x - mu
    var = jnp.mean(xc * xc, axis=-1, keepdims=True)
    return xc * lax.rsqrt(var + LN_EPS) * g + b


def _tri(n, strict=False, upper=False):
    r = lax.broadcasted_iota(jnp.int32, (n, n), 0)
    c = lax.broadcasted_iota(jnp.int32, (n, n), 1)
    if upper:
        r, c = c, r
    return (c < r) if strict else (c <= r)


def _linear_kernel(x_ref, w_ref, o_ref):
    o_ref[...] = _dot(x_ref[...], w_ref[...]).astype(o_ref.dtype)


def _linear(x, w, out_dtype, tm, tn, name):
    t, k = x.shape
    n = w.shape[1]
    assert t % tm == 0 and n % tn == 0
    return pl.pallas_call(
        _linear_kernel,
        grid=(t // tm, n // tn),
        in_specs=[pl.BlockSpec((tm, k), lambda i, j: (i, 0)),
                  pl.BlockSpec((k, tn), lambda i, j: (0, j))],
        out_specs=pl.BlockSpec((tm, tn), lambda i, j: (i, j)),
        out_shape=jax.ShapeDtypeStruct((t, n), out_dtype),
        compiler_params=_params(("parallel", "arbitrary")),
        name=name,
    )(x, w)


def _linear_f32_kernel(x_ref, w_ref, o_ref):
    o_ref[...] = _dot_f32(x_ref[...], w_ref[...])


def _linear_f32(x, w, tm, name):
    t, k = x.shape
    n = w.shape[1]
    return pl.pallas_call(
        _linear_f32_kernel,
        grid=(t // tm,),
        in_specs=[pl.BlockSpec((tm, k), lambda i: (i, 0)),
                  pl.BlockSpec((k, n), lambda i: (0, 0))],
        out_specs=pl.BlockSpec((tm, n), lambda i: (i, 0)),
        out_shape=jax.ShapeDtypeStruct((t, n), F32),
        compiler_params=_params(("parallel",)),
        name=name,
    )(x, w)


def _out_ln_kernel(y_ref, w_ref, res_ref, g_ref, b_ref, o_ref):
    acc = _dot(y_ref[...], w_ref[...])
    o_ref[...] = _layer_norm(DEEPNORM_ALPHA * res_ref[...] + acc, g_ref[...], b_ref[...])


def _out_ln(y, w, res, g, b, tm, name):
    t, k = y.shape
    d = w.shape[1]
    return pl.pallas_call(
        _out_ln_kernel,
        grid=(t // tm,),
        in_specs=[pl.BlockSpec((tm, k), lambda i: (i, 0)),
                  pl.BlockSpec((k, d), lambda i: (0, 0)),
                  pl.BlockSpec((tm, d), lambda i: (i, 0)),
                  pl.BlockSpec((1, d), lambda i: (0, 0)),
                  pl.BlockSpec((1, d), lambda i: (0, 0))],
        out_specs=pl.BlockSpec((tm, d), lambda i: (i, 0)),
        out_shape=jax.ShapeDtypeStruct((t, d), F32),
        compiler_params=_params(("parallel",)),
        name=name,
    )(y, w, res, g.reshape(1, d), b.reshape(1, d))


def _ffn_kernel(x_ref, wg_ref, wu_ref, wo_ref, g_ref, b_ref, o_ref, acc_ref):
    j = pl.program_id(1)
    x = x_ref[...]
    xb = x.astype(BF16)
    gate = _dot(xb, wg_ref[...])
    up = _dot(xb, wu_ref[...])
    part = _dot(_silu(gate) * up, wo_ref[...])

    @pl.when(j == 0)
    def _():
        acc_ref[...] = part

    @pl.when(j > 0)
    def _():
        acc_ref[...] += part

    @pl.when(j == pl.num_programs(1) - 1)
    def _():
        o_ref[...] = _layer_norm(DEEPNORM_ALPHA * x + acc_ref[...], g_ref[...], b_ref[...])


def _ffn_ln(x, w_in, w_out, g, b, tm, tf):
    t, d = x.shape
    f = w_out.shape[0]
    nf = f // tf
    return pl.pallas_call(
        _ffn_kernel,
        grid=(t // tm, nf),
        in_specs=[pl.BlockSpec((tm, d), lambda i, j: (i, 0)),
                  pl.BlockSpec((d, tf), lambda i, j: (0, j)),
                  pl.BlockSpec((d, tf), lambda i, j: (0, j + nf)),
                  pl.BlockSpec((tf, d), lambda i, j: (j, 0)),
                  pl.BlockSpec((1, d), lambda i, j: (0, 0)),
                  pl.BlockSpec((1, d), lambda i, j: (0, 0))],
        out_specs=pl.BlockSpec((tm, d), lambda i, j: (i, 0)),
        out_shape=jax.ShapeDtypeStruct((t, d), F32),
        scratch_shapes=[pltpu.VMEM((tm, d), F32)],
        compiler_params=_params(("parallel", "arbitrary")),
        name="ffn_ln",
    )(x, w_in, w_in, w_out, g.reshape(1, d), b.reshape(1, d))


def _rows_to_chunk_rows(a, bsz, nc, l):
    c = a.shape[1]
    return a.reshape(bsz, nc, l, c).transpose(0, 1, 3, 2)


def _mlstm_kernel(q_ref, k_ref, v_ref, o_ref, gc_ref, gr_ref, bc_ref, br_ref, nw_ref,
                  out_ref, c_ref, n_ref, m_ref, *, l):
    h_, dk, dv = ML_HEADS, ML_DQK, ML_DV

    @pl.when(pl.program_id(1) == 0)
    def _():
        c_ref[...] = jnp.zeros_like(c_ref)
        n_ref[...] = jnp.zeros_like(n_ref)
        m_ref[...] = jnp.zeros_like(m_ref)

    cap = lambda t: ML_GATE_CAP * jnp.tanh(t / ML_GATE_CAP)
    gc = cap(gc_ref[...] + bc_ref[...])
    gr = cap(gr_ref[...] + br_ref[...])
    i_col, lf_col = gc[:, :h_], -_softplus(-gc[:, h_:])
    i_row, lf_row = gr[:h_, :], -_softplus(-gr[h_:, :])
    causal = _tri(l)
    b_col = _sel_dot(causal, lf_col)
    b_row = _dot_sel(lf_row, _tri(l, upper=True))
    scale = dk ** -0.5

    for h in range(h_):
        q = q_ref[:, h * dk:(h + 1) * dk]
        k = k_ref[:, h * dk:(h + 1) * dk]
        v = v_ref[:, h * dv:(h + 1) * dv]
        bc, br = b_col[:, h:h + 1], b_row[h:h + 1, :]
        ic, ir = i_col[:, h:h + 1], i_row[h:h + 1, :]
        m_prev = m_ref[h][:, :1]
        c_prev = c_ref[h]
        n_prev = n_ref[h]

        dmat = jnp.where(causal, bc - br + ir, -jnp.inf)
        inter = bc + m_prev
        m_t = jnp.maximum(inter, jnp.max(dmat, axis=1, keepdims=True))
        wts = jnp.exp(dmat - m_t)
        sc = jnp.exp(inter - m_t) * scale
        qk = _dot(q, k, ((1,), (1,))) * (wts * scale)
        num = _dot(qk, v) + sc * _dot(q, c_prev)
        den = (jnp.sum(qk, axis=1, keepdims=True)
               + sc * jnp.sum(q.astype(F32) * n_prev, axis=1, keepdims=True))
        hh = num / jnp.maximum(jnp.abs(den), jnp.exp(-m_t))

        b_last = bc[l - 1:l, :]
        g_row = b_last - br + ir
        g_col = b_last - bc + ic
        m_new = jnp.maximum(b_last + m_prev, jnp.max(g_row, axis=1, keepdims=True))
        kw = k.astype(F32) * jnp.exp(g_col - m_new)
        dec = jnp.exp(b_last + m_prev - m_new)
        c_ref[h] = dec * c_prev + _dot(kw, v, ((0,), (0,)))
        n_ref[h] = dec * n_prev + jnp.sum(kw, axis=0, keepdims=True)
        m_ref[h] = jnp.broadcast_to(m_new, (1, LANES))

        hn = hh * lax.rsqrt(jnp.mean(hh * hh, axis=1, keepdims=True) + RMS_EPS)
        sl = slice(h * dv, (h + 1) * dv)
        out_ref[:, sl] = (hn * nw_ref[:, sl] * _sigmoid(o_ref[:, sl].astype(F32))
                          ).astype(out_ref.dtype)


def _mlstm_core(proj, gpre, b_gate, norm_w, bsz, seq, l):
    h_, dk, dv = ML_HEADS, ML_DQK, ML_DV
    nc = seq // l
    t = bsz * seq
    row = lambda b, c: (b * nc + c, 0)
    g_rows = _rows_to_chunk_rows(gpre, bsz, nc, l)
    return pl.pallas_call(
        functools.partial(_mlstm_kernel, l=l),
        grid=(bsz, nc),
        in_specs=[pl.BlockSpec((l, h_ * dk), lambda b, c: (b * nc + c, 0)),
                  pl.BlockSpec((l, h_ * dk), lambda b, c: (b * nc + c, 1)),
                  pl.BlockSpec((l, h_ * dv), lambda b, c: (b * nc + c, 1)),
                  pl.BlockSpec((l, h_ * dv), lambda b, c: (b * nc + c, 2)),
                  pl.BlockSpec((l, 2 * h_), row),
                  pl.BlockSpec((None, None, 2 * h_, l), lambda b, c: (b, c, 0, 0)),
                  pl.BlockSpec((1, 2 * h_), lambda b, c: (0, 0)),
                  pl.BlockSpec((2 * h_, 1), lambda b, c: (0, 0)),
                  pl.BlockSpec((1, h_ * dv), lambda b, c: (0, 0))],
        out_specs=pl.BlockSpec((l, h_ * dv), row),
        out_shape=jax.ShapeDtypeStruct((t, h_ * dv), BF16),
        scratch_shapes=[pltpu.VMEM((h_, dk, dv), F32),
                        pltpu.VMEM((h_, 1, dk), F32),
                        pltpu.VMEM((h_, 1, LANES), F32)],
        compiler_params=_params(("parallel", "arbitrary")),
        name="mlstm_core",
    )(proj, proj, proj, proj, gpre, g_rows, b_gate.reshape(1, 2 * h_),
      b_gate.reshape(2 * h_, 1), norm_w.reshape(1, h_ * dv))


def _pad_cols(w, n):
    return jnp.pad(w, ((0, 0), (0, n - w.shape[1])))


def _mlstm_layer(x, res, w_in, b_gate, norm_w, w_out, ln_g, ln_b, bsz, seq):
    proj = _linear(x, w_in[:, :ML_QKVO].astype(BF16), BF16, 512, 1024, "mlstm_in")
    gpre = _linear_f32(x, _pad_cols(w_in[:, ML_QKVO:], LANES), 512, "mlstm_gates")
    hn = _mlstm_core(proj, gpre[:, :2 * ML_HEADS], b_gate, norm_w, bsz, seq, ML_CHUNK)
    return _out_ln(hn, w_out.astype(BF16), res, ln_g, ln_b, 512, "mlstm_out_ln")


def _ssd_kernel(z_ref, x_ref, bm_ref, cm_ref, dtc_ref, dtr_ref,
                cwx_ref, cwb_ref, cwc_ref, cbx_ref, cbb_ref, cbc_ref,
                dbc_ref, dbr_ref, alc_ref, alr_ref, dsk_ref, nw_ref,
                out_ref, st_ref, px_ref, pb_ref, pc_ref, y_ref, *, l):
    hpg, p, n = SSM_HPG, SSM_HEADDIM, SSM_STATE
    tail = SUBLANES

    @pl.when(pl.program_id(2) == 0)
    def _():
        st_ref[...] = jnp.zeros_like(st_ref)
        px_ref[0:tail, :] = jnp.zeros((tail, px_ref.shape[1]), F32)
        pb_ref[0:tail, :] = jnp.zeros((tail, n), F32)
        pc_ref[0:tail, :] = jnp.zeros((tail, n), F32)

    def conv_silu(pad_ref, cur_ref, w_ref, b_ref):
        pad_ref[tail:tail + l, :] = cur_ref[...].astype(F32)
        acc = b_ref[...] + w_ref[0:1, :] * pad_ref[tail - 3:tail - 3 + l, :]
        for j in range(1, SSM_CONV):
            off = tail - (SSM_CONV - 1) + j
            acc = acc + w_ref[j:j + 1, :] * pad_ref[off:off + l, :]
        pad_ref[0:tail, :] = pad_ref[l:l + tail, :]
        return _silu(acc)

    xs = conv_silu(px_ref, x_ref, cwx_ref, cbx_ref)
    bg = conv_silu(pb_ref, bm_ref, cwb_ref, cbb_ref)
    cg = conv_silu(pc_ref, cm_ref, cwc_ref, cbc_ref)

    dt_c = _softplus(dtc_ref[...] + dbc_ref[...])
    dt_r = _softplus(dtr_ref[...] + dbr_ref[...])
    causal = _tri(l)
    cum_c = _sel_dot(causal, dt_c * -jnp.exp(alc_ref[...]))
    cum_r = _dot_sel(dt_r * -jnp.exp(alr_ref[...]), _tri(l, upper=True))
    cb = _dot(cg, bg, ((1,), (1,)))
    bgb = bg.astype(BF16)
    cgb = cg.astype(BF16)

    for j in range(hpg):
        cc, cr = cum_c[:, j:j + 1], cum_r[j:j + 1, :]
        xj = xs[:, j * p:(j + 1) * p]
        st = st_ref[j]
        mm = cb * jnp.exp(jnp.where(causal, cc - cr, -jnp.inf)) * dt_r[j:j + 1, :]
        y = _dot(mm, xj) + _dot(cgb, st, ((1,), (1,))) * jnp.exp(cc)
        c_last = cc[l - 1:l, :]
        w_s = jnp.exp(c_last - cc) * dt_c[:, j:j + 1]
        st_ref[j] = st * jnp.exp(c_last) + _dot(xj * w_s, bgb, ((0,), (0,)))
        y_ref[:, j * p:(j + 1) * p] = y + xj * dsk_ref[:, j * p:(j + 1) * p]

    y = y_ref[...] * _silu(z_ref[...].astype(F32))
    y = y * lax.rsqrt(jnp.mean(y * y, axis=1, keepdims=True) + RMS_EPS)
    out_ref[...] = (y * nw_ref[...]).astype(out_ref.dtype)


def _ssd_core(proj, dtpre, conv_w, conv_b, dt_bias, a_log, d_skip, norm_w, bsz, seq, l):
    g_, hpg, p, n, gw = SSM_GROUPS, SSM_HPG, SSM_HEADDIM, SSM_STATE, SSM_GW
    di = SSM_DINNER
    nc = seq // l
    t = bsz * seq
    dt_cols = dtpre.reshape(t, g_, hpg).transpose(1, 0, 2)
    dt_rows = dt_cols.reshape(g_, bsz, nc, l, hpg).transpose(0, 1, 2, 4, 3)
    per_head = lambda a: a.reshape(g_, 1, hpg)
    d_cols = jnp.repeat(d_skip, p).reshape(1, di)
    row = lambda b, g, c: b * nc + c
    xoff, boff, coff = di // gw, 2 * di // n, 2 * di // n + g_
    grp_vec = lambda w: pl.BlockSpec((None, 1, hpg), lambda b, g, c: (g, 0, 0))
    grp_vec_t = lambda w: pl.BlockSpec((None, hpg, 1), lambda b, g, c: (g, 0, 0))
    cwx, cwb, cwc = conv_w[:, :di], conv_w[:, di:di + g_ * n], conv_w[:, di + g_ * n:]
    cb2 = conv_b.reshape(1, -1)
    cbx, cbb, cbc = cb2[:, :di], cb2[:, di:di + g_ * n], cb2[:, di + g_ * n:]
    return pl.pallas_call(
        functools.partial(_ssd_kernel, l=l),
        grid=(bsz, g_, nc),
        in_specs=[pl.BlockSpec((l, gw), lambda b, g, c: (row(b, g, c), g)),
                  pl.BlockSpec((l, gw), lambda b, g, c: (row(b, g, c), xoff + g)),
                  pl.BlockSpec((l, n), lambda b, g, c: (row(b, g, c), boff + g)),
                  pl.BlockSpec((l, n), lambda b, g, c: (row(b, g, c), coff + g)),
                  pl.BlockSpec((None, l, hpg), lambda b, g, c: (g, row(b, g, c), 0)),
                  pl.BlockSpec((None, None, None, hpg, l), lambda b, g, c: (g, b, c, 0, 0)),
                  pl.BlockSpec((SSM_CONV, gw), lambda b, g, c: (0, g)),
                  pl.BlockSpec((SSM_CONV, n), lambda b, g, c: (0, g)),
                  pl.BlockSpec((SSM_CONV, n), lambda b, g, c: (0, g)),
                  pl.BlockSpec((1, gw), lambda b, g, c: (0, g)),
                  pl.BlockSpec((1, n), lambda b, g, c: (0, g)),
                  pl.BlockSpec((1, n), lambda b, g, c: (0, g)),
                  grp_vec(None), grp_vec_t(None), grp_vec(None), grp_vec_t(None),
                  pl.BlockSpec((1, gw), lambda b, g, c: (0, g)),
                  pl.BlockSpec((1, gw), lambda b, g, c: (0, g))],
        out_specs=pl.BlockSpec((l, gw), lambda b, g, c: (row(b, g, c), g)),
        out_shape=jax.ShapeDtypeStruct((t, di), BF16),
        scratch_shapes=[pltpu.VMEM((hpg, p, n), F32),
                        pltpu.VMEM((l + SUBLANES, gw), F32),
                        pltpu.VMEM((l + SUBLANES, n), F32),
                        pltpu.VMEM((l + SUBLANES, n), F32),
                        pltpu.VMEM((l, gw), F32)],
        compiler_params=_params(("parallel", "parallel", "arbitrary")),
        name="ssd_core",
    )(proj, proj, proj, proj, dt_cols, dt_rows, cwx, cwb, cwc, cbx, cbb, cbc,
      per_head(dt_bias), per_head(dt_bias).transpose(0, 2, 1),
      per_head(a_log), per_head(a_log).transpose(0, 2, 1),
      d_cols, norm_w.reshape(1, di))


def _mamba_layer(x, res, w_in, conv_w, conv_b, dt_bias, a_log, d_skip, norm_w, w_out,
                 ln_g, ln_b, bsz, seq):
    proj = _linear(x, w_in[:, :SSM_ZX].astype(BF16), BF16, 512, 1024, "ssm_in")
    dtpre = _linear_f32(x, _pad_cols(w_in[:, SSM_ZX:], LANES), 512, "ssm_dt")
    y = _ssd_core(proj, dtpre[:, :SSM_HEADS], conv_w, conv_b, dt_bias, a_log, d_skip,
                  norm_w, bsz, seq, SSM_CHUNK)
    return _out_ln(y, w_out.astype(BF16), res, ln_g, ln_b, 512, "ssm_out_ln")


def _rwkv_prep_kernel(x_ref, xp_ref, mix_ref, wrkv_ref, w0_ref, w1_ref, w2_ref,
                      a0_ref, a1_ref, a2_ref, g1_ref, g2_ref,
                      r_ref, k_ref, v_ref, lw_ref, a_ref, g_ref):
    x = x_ref[...]
    xx = xp_ref[...] - x
    mixed = lambda j: x + xx * mix_ref[j:j + 1, :]
    r_ref[...] = _dot(mixed(0), wrkv_ref[0])
    k_ref[...] = _dot(mixed(2), wrkv_ref[1])
    v_ref[...] = _dot(mixed(3), wrkv_ref[2])
    w_raw = w0_ref[...] + _dot(jnp.tanh(_dot(mixed(1), w1_ref[...])), w2_ref[...])
    w = -_softplus(-w_raw) - 0.5
    lw_ref[...] = -jnp.exp(w)
    a_ref[...] = _sigmoid(a0_ref[...] + _dot(_dot(mixed(4), a1_ref[...]), a2_ref[...]))
    g_ref[...] = _dot(_sigmoid(_dot(mixed(5), g1_ref[...])), g2_ref[...])


def _rwkv_prep(x, xprev, mix, w_rkv, w0, w1, w2, a0, a1, a2, g1, g2, tm):
    t, d = x.shape
    full = lambda a: pl.BlockSpec(a.shape, lambda i: (0,) * a.ndim)
    rowblk = pl.BlockSpec((tm, d), lambda i: (i, 0))
    ws = [mix, w_rkv.astype(BF16), w0.reshape(1, d), w1.astype(BF16), w2.astype(BF16),
          a0.reshape(1, d), a1.astype(BF16), a2.astype(BF16), g1.astype(BF16),
          g2.astype(BF16)]
    return pl.pallas_call(
        _rwkv_prep_kernel,
        grid=(t // tm,),
        in_specs=[rowblk, rowblk] + [full(a) for a in ws],
        out_specs=[rowblk] * 6,
        out_shape=[jax.ShapeDtypeStruct((t, d), F32)] * 6,
        compiler_params=_params(("parallel",)),
        name="rwkv_prep",
    )(x, xprev, *ws)


def _rwkv_scan_kernel(r_ref, k_ref, v_ref, lw_ref, a_ref, g_ref,
                      kk_ref, ka_ref, rk_ref, lnw_ref, lnb_ref,
                      out_ref, s_ref, *, l):
    hd = RW_HEADDIM
    assert l == hd and 2 * hd == LANES
    n_pairs = r_ref.shape[1] // LANES

    @pl.when(pl.program_id(1) == 0)
    def _():
        s_ref[...] = jnp.zeros_like(s_ref)

    incl = _tri(l)
    lane = lax.broadcasted_iota(jnp.int32, (l, LANES), 1)
    tok_t = lax.broadcasted_iota(jnp.int32, (l, LANES), 0)
    tok_s = lane & (hd - 1)
    lo = lane < hd
    strict2 = tok_s < tok_t
    incl2 = tok_s <= tok_t
    mlo = lambda x: jnp.where(lo, x, 0.0)
    mhi = lambda x: jnp.where(lo, 0.0, x)
    swap = lambda x: pltpu.roll(x, hd, axis=1)
    cat0 = lambda *xs: jnp.concatenate(xs, axis=0)
    nt = ((1,), (1,))
    tn = ((0,), (0,))

    def half_sums(x):
        s_lo = jnp.sum(mlo(x), axis=1, keepdims=True)
        s_hi = jnp.sum(mhi(x), axis=1, keepdims=True)
        return jnp.where(lo, s_lo, s_hi)

    pairs = range(n_pairs)
    cols = [slice(p * LANES, (p + 1) * LANES) for p in pairs]
    steps = l.bit_length() - 1

    st = []
    for p in pairs:
        lw = lw_ref[:, cols[p]]
        a_sig = a_ref[:, cols[p]]
        k_in = k_ref[:, cols[p]]
        gcum = _sel_dot(incl, lw)
        g_last = gcum[l - 1:l, :]
        p_inv = jnp.exp(-gcum)
        p_tail = jnp.exp(g_last - gcum)
        kmod = k_in * (1.0 + (a_sig - 1.0) * ka_ref[:, cols[p]])
        kk = k_in * kk_ref[:, cols[p]]
        kk = kk * lax.rsqrt(jnp.maximum(half_sums(kk * kk), 1e-24))
        b_vec = kk * a_sig
        a_dec = -kk * jnp.exp(gcum - lw)
        r_dec = r_ref[:, cols[p]] * jnp.exp(gcum)
        pm = _dot(cat0(mlo(a_dec), mlo(r_dec), mhi(a_dec), mhi(r_dec)),
                  cat0(b_vec * p_inv, kmod * p_inv), nt)
        a_e = jnp.where(strict2, pm[0 * l:1 * l], 0.0)
        a_o = jnp.where(strict2, pm[2 * l:3 * l], 0.0)
        m_eo = cat0(jnp.where(incl2, pm[1 * l:2 * l], 0.0),
                    jnp.where(incl2, pm[3 * l:4 * l], 0.0)).astype(BF16)
        v_sw = swap(v_ref[:, cols[p]])
        akv = _dot(cat0(a_e, a_o), cat0(jnp.zeros_like(v_sw), v_sw))
        xx = cat0(mlo(a_dec) + mhi(akv[:l]), mhi(a_dec) + mlo(akv[l:]))
        n_bd = cat0(mlo(a_e), swap(mlo(a_o)))
        st.append(dict(xx=xx, n_bd=n_bd, m_eo=m_eo, v_sw=v_sw, r_dec=r_dec,
                       bk_tail=cat0(b_vec * p_tail, kmod * p_tail).astype(BF16),
                       p_last=jnp.exp(g_last), kmod=kmod))

    for j in range(steps):
        for s in st:
            nb = s["n_bd"].astype(BF16)
            if j + 1 < steps:
                out = _dot(nb, jnp.concatenate([s["xx"].astype(BF16), nb], axis=1))
                s["xx"] = s["xx"] + out[:, :LANES]
                s["n_bd"] = out[:, LANES:]
            else:
                s["xx"] = s["xx"] + _dot(nb, s["xx"])

    for p, s in zip(pairs, st):
        x_e, x_o = s["xx"][:l], s["xx"][l:]
        r_dec = s["r_dec"]
        s_pair = s_ref[p]
        tt = _dot(cat0(mlo(x_e), mlo(r_dec), mhi(x_o), mhi(r_dec)),
                  cat0(s_pair, s_pair), nt)
        uu = mhi(tt[0 * l:1 * l] + x_e) + mlo(tt[2 * l:3 * l] + x_o)
        y0 = mhi(tt[1 * l:2 * l]) + mlo(tt[3 * l:4 * l])
        uv = cat0(uu, s["v_sw"]).astype(BF16)
        yc = _dot(s["m_eo"], uv)
        s["y"] = swap(y0 + mhi(yc[:l]) + mlo(yc[l:]))
        ds = _dot(uv, s["bk_tail"], tn)
        s_ref[p] = s_pair * s["p_last"] + jnp.where(lo, ds[l:], ds[:l])

    for p, s in zip(pairs, st):
        y = s["y"]
        inv_hd = 1.0 / hd
        mu = half_sums(y) * inv_hd
        yc = y - mu
        var = half_sums(yc * yc) * inv_hd
        yn = yc * lax.rsqrt(var + RW_LNX_EPS) * lnw_ref[:, cols[p]] + lnb_ref[:, cols[p]]
        bonus = half_sums(r_ref[:, cols[p]] * s["kmod"] * rk_ref[:, cols[p]])
        yn = yn + bonus * v_ref[:, cols[p]]
        out_ref[:, cols[p]] = (yn * g_ref[:, cols[p]]).astype(out_ref.dtype)


def _rwkv_scan(r, k, v, lw, a, g, k_k, k_a, r_k, lnx_w, lnx_b, bsz, seq, l):
    t, d = r.shape
    nc = seq // l
    blk = pl.BlockSpec((l, d), lambda b, c: (b * nc + c, 0))
    vec = pl.BlockSpec((1, d), lambda b, c: (0, 0))
    vecs = [a_.reshape(1, d) for a_ in (k_k, k_a, r_k, lnx_w, lnx_b)]
    return pl.pallas_call(
        functools.partial(_rwkv_scan_kernel, l=l),
        grid=(bsz, nc),
        in_specs=[blk] * 6 + [vec] * 5,
        out_specs=blk,
        out_shape=jax.ShapeDtypeStruct((t, d), BF16),
        scratch_shapes=[pltpu.VMEM((d // LANES, RW_HEADDIM, LANES), F32)],
        compiler_params=_params(("parallel", "arbitrary")),
        name="rwkv_scan",
    )(r, k, v, lw, a, g, *vecs)


def _rwkv_layer(x, res, mix, w_rkv, w0, w1, w2, a0, a1, a2, g1, g2, k_k, k_a, r_k,
                lnx_w, lnx_b, w_out, ln_g, ln_b, bsz, seq):
    d = x.shape[1]
    x3 = x.reshape(bsz, seq, d)
    xprev = jnp.pad(x3, ((0, 0), (1, 0), (0, 0)))[:, :-1].reshape(bsz * seq, d)
    r, k, v, lw, a, g = _rwkv_prep(x, xprev, mix, w_rkv, w0, w1, w2, a0, a1, a2, g1, g2, 256)
    y = _rwkv_scan(r, k, v, lw, a, g, k_k, k_a, r_k, lnx_w, lnx_b, bsz, seq, RW_CHUNK)
    return _out_ln(y, w_out.astype(BF16), res, ln_g, ln_b, 512, "rwkv_out_ln")


def kernel(x, ln_g, ln_b, ffn_w_in, ffn_w_out, ml_w_in, ml_b_gate, ml_norm_w, ml_w_out, ssm_w_in, ssm_conv_w, ssm_conv_b, ssm_dt_bias, ssm_a_log, ssm_d, ssm_norm_w, ssm_w_out, rw_mix, rw_w_rkv, rw_w0, rw_w1, rw_w2, rw_a0, rw_a1, rw_a2, rw_g1, rw_g2, rw_k_k, rw_k_a, rw_r_k, rw_lnx_w, rw_lnx_b, rw_w_out):
    bsz, seq, d = x.shape
    h = x.reshape(bsz * seq, d)
    for i in range(DEPTH):
        kind, j = i % N_MIXERS, i // N_MIXERS
        if kind == 0:
            h = _mlstm_layer(h, h, ml_w_in[j], ml_b_gate[j], ml_norm_w[j], ml_w_out[j],
                             ln_g[i, 0], ln_b[i, 0], bsz, seq)
        elif kind == 1:
            h = _mamba_layer(h, h, ssm_w_in[j], ssm_conv_w[j], ssm_conv_b[j], ssm_dt_bias[j],
                             ssm_a_log[j], ssm_d[j], ssm_norm_w[j], ssm_w_out[j],
                             ln_g[i, 0], ln_b[i, 0], bsz, seq)
        else:
            h = _rwkv_layer(h, h, rw_mix[j], rw_w_rkv[j], rw_w0[j], rw_w1[j], rw_w2[j],
                            rw_a0[j], rw_a1[j], rw_a2[j], rw_g1[j], rw_g2[j], rw_k_k[j],
                            rw_k_a[j], rw_r_k[j].reshape(-1), rw_lnx_w[j], rw_lnx_b[j],
                            rw_w_out[j], ln_g[i, 0], ln_b[i, 0], bsz, seq)
        h = _ffn_ln(h, ffn_w_in[i].astype(BF16), ffn_w_out[i].astype(BF16),
                    ln_g[i, 1], ln_b[i, 1], 512, 1408)
    return h.reshape(bsz, seq, d)
```

```python
import functools
import math

import jax
import jax.numpy as jnp
from jax import lax
from jax.experimental import pallas as pl
from jax.experimental.pallas import tpu as pltpu

F32 = jnp.float32
BF16 = jnp.bfloat16

D_MODEL = 1024
DEPTH = 4
N_MIXERS = 3
DEEPNORM_ALPHA = (2 * DEPTH) ** 0.25
LN_EPS = 1e-5
RMS_EPS = 1e-6

ML_HEADS = 4
ML_DV = D_MODEL // ML_HEADS
ML_DQK = ML_DV // 2
ML_GATE_CAP = 15.0
ML_QKVO = 2 * ML_HEADS * ML_DQK + 2 * ML_HEADS * ML_DV

SSM_DINNER = 2 * D_MODEL
SSM_HEADDIM = 64
SSM_HEADS = SSM_DINNER // SSM_HEADDIM
SSM_STATE = 128
SSM_GROUPS = 4
SSM_CONV = 4
SSM_HPG = SSM_HEADS // SSM_GROUPS
SSM_GW = SSM_DINNER // SSM_GROUPS
SSM_ZX = 2 * SSM_DINNER + 2 * SSM_GROUPS * SSM_STATE

RW_HEADDIM = 64
RW_HEADS = D_MODEL // RW_HEADDIM
RW_LNX_EPS = 64e-5

FFN_HIDDEN = int(math.ceil(8 * D_MODEL / 3 / 256)) * 256

LANES = 128
SUBLANES = 8
VMEM_LIMIT = 56 * 1024 * 1024

ML_CHUNK = 256
SSM_CHUNK = 128
SSM_TAIL = 2 * SUBLANES
RW_CHUNK = 64


def _params(sem):
    return pltpu.CompilerParams(dimension_semantics=sem, vmem_limit_bytes=VMEM_LIMIT)


def _dot(a, b, dims=((1,), (0,))):
    return lax.dot_general(a.astype(BF16), b.astype(BF16), (dims, ((), ())),
                           preferred_element_type=F32)


def _split3(a):
    hi = a.astype(BF16)
    r1 = a - hi.astype(F32)
    mid = r1.astype(BF16)
    lo = (r1 - mid.astype(F32)).astype(BF16)
    return hi, mid, lo


def _dot_sel(a, sel, dims=((1,), (0,))):
    sel = sel.astype(BF16)
    return sum(lax.dot_general(p, sel, (dims, ((), ())), preferred_element_type=F32)
               for p in _split3(a))


def _sel_dot(sel, a, dims=((1,), (0,))):
    sel = sel.astype(BF16)
    return sum(lax.dot_general(sel, p, (dims, ((), ())), preferred_element_type=F32)
               for p in _split3(a))


def _softplus(x):
    return jnp.maximum(x, 0.0) + jnp.log1p(jnp.exp(-jnp.abs(x)))


def _sigmoid(x):
    return 0.5 * jnp.tanh(0.5 * x) + 0.5


def _silu(x):
    return x * _sigmoid(x)


def _layer_norm(x, g, b):
    mu = jnp.mean(x, axis=-1, keepdims=True)
    xc = x - mu
    var = jnp.mean(xc * xc, axis=-1, keepdims=True)
    return xc * lax.rsqrt(var + LN_EPS) * g + b


def _tri(n, strict=False, upper=False):
    r = lax.broadcasted_iota(jnp.int32, (n, n), 0)
    c = lax.broadcasted_iota(jnp.int32, (n, n), 1)
    if upper:
        r, c = c, r
    return (c < r) if strict else (c <= r)


def _resident(a):
    return pl.BlockSpec(a.shape, lambda *_: (0,) * a.ndim, pipeline_mode=pl.Buffered(1))


def _in_proj_kernel(x_ref, w_ref, ws_ref, o_ref, os_ref, *, tn):
    x = x_ref[...]
    xb = x.astype(BF16)
    for j in range(w_ref.shape[1] // tn):
        cols = slice(j * tn, (j + 1) * tn)
        o_ref[:, cols] = _dot(xb, w_ref[:, cols]).astype(o_ref.dtype)
    x_lo = (x - xb.astype(F32)).astype(BF16)
    both = _dot(xb, ws_ref[...])
    os_ref[...] = both[:, :LANES] + both[:, LANES:] + _dot(x_lo, ws_ref[:, :LANES])


def _in_proj(x, w, n_main, tm, tn, name):
    t, k = x.shape
    assert t % tm == 0 and n_main % tn == 0 and w.shape[1] - n_main <= LANES
    w_main = w[:, :n_main].astype(BF16)
    w_few = jnp.pad(w[:, n_main:], ((0, 0), (0, LANES - (w.shape[1] - n_main))))
    w_hi = w_few.astype(BF16)
    w_lo = (w_few - w_hi.astype(F32)).astype(BF16)
    w_split = jnp.concatenate([w_hi, w_lo], axis=1)
    return pl.pallas_call(
        functools.partial(_in_proj_kernel, tn=tn),
        grid=(t // tm,),
        in_specs=[pl.BlockSpec((tm, k), lambda i: (i, 0)), _resident(w_main), _resident(w_split)],
        out_specs=[pl.BlockSpec((tm, n_main), lambda i: (i, 0)),
                   pl.BlockSpec((tm, LANES), lambda i: (i, 0))],
        out_shape=[jax.ShapeDtypeStruct((t, n_main), BF16),
                   jax.ShapeDtypeStruct((t, LANES), F32)],
        compiler_params=_params(("parallel",)),
        name=name,
    )(x, w_main, w_split)


def _out_ln_kernel(y_ref, w_ref, res_ref, g_ref, b_ref, o_ref):
    acc = _dot(y_ref[...], w_ref[...])
    o_ref[...] = _layer_norm(DEEPNORM_ALPHA * res_ref[...] + acc, g_ref[...], b_ref[...])


def _out_ln(y, w, res, g, b, tm, name):
    t, k = y.shape
    d = w.shape[1]
    g2, b2 = g.reshape(1, d), b.reshape(1, d)
    return pl.pallas_call(
        _out_ln_kernel,
        grid=(t // tm,),
        in_specs=[pl.BlockSpec((tm, k), lambda i: (i, 0)),
                  _resident(w),
                  pl.BlockSpec((tm, d), lambda i: (i, 0)),
                  _resident(g2), _resident(b2)],
        out_specs=pl.BlockSpec((tm, d), lambda i: (i, 0)),
        out_shape=jax.ShapeDtypeStruct((t, d), F32),
        compiler_params=_params(("parallel",)),
        name=name,
    )(y, w, res, g2, b2)


def _ffn_kernel(x_ref, wi_ref, wo_ref, g_ref, b_ref, o_ref, *, tf):
    x = x_ref[...]
    xb = x.astype(BF16)
    f = wo_ref.shape[0]
    acc = None
    for j in range(f // tf):
        gate = _dot(xb, wi_ref[:, j * tf:(j + 1) * tf])
        up = _dot(xb, wi_ref[:, f + j * tf:f + (j + 1) * tf])
        part = _dot(_silu(gate) * up, wo_ref[j * tf:(j + 1) * tf, :])
        acc = part if acc is None else acc + part
    o_ref[...] = _layer_norm(DEEPNORM_ALPHA * x + acc, g_ref[...], b_ref[...])


def _ffn_ln(x, w_in, w_out, g, b, tm, tf):
    t, d = x.shape
    f = w_out.shape[0]
    assert f % tf == 0 and tf % LANES == 0
    g2, b2 = g.reshape(1, d), b.reshape(1, d)
    return pl.pallas_call(
        functools.partial(_ffn_kernel, tf=tf),
        grid=(t // tm,),
        in_specs=[pl.BlockSpec((tm, d), lambda i: (i, 0)),
                  _resident(w_in), _resident(w_out), _resident(g2), _resident(b2)],
        out_specs=pl.BlockSpec((tm, d), lambda i: (i, 0)),
        out_shape=jax.ShapeDtypeStruct((t, d), F32),
        compiler_params=_params(("parallel",)),
        name="ffn_ln",
    )(x, w_in, w_out, g2, b2)


def _rows_to_chunk_rows(a, bsz, nc, l):
    c = a.shape[1]
    return a.reshape(bsz, nc, l, c).transpose(0, 1, 3, 2)


def _mlstm_kernel(q_ref, k_ref, v_ref, o_ref, gc_ref, gr_ref, bc_ref, br_ref, nw_ref,
                  out_ref, c_ref, n_ref, m_ref, *, l):
    h_, dk, dv = ML_HEADS, ML_DQK, ML_DV

    cur = pl.program_id(1) % 2
    nxt = 1 - cur

    @pl.when(pl.program_id(1) == 0)
    def _():
        c_ref[0] = jnp.zeros(c_ref.shape[1:], F32)
        n_ref[0] = jnp.zeros(n_ref.shape[1:], F32)
        m_ref[0] = jnp.zeros(m_ref.shape[1:], F32)

    cap = lambda t: ML_GATE_CAP * jnp.tanh(t / ML_GATE_CAP)
    gc = cap(gc_ref[...] + bc_ref[...])
    gr = cap(gr_ref[...] + br_ref[...])
    i_col, lf_col = gc[:, :h_], -_softplus(-gc[:, h_:])
    i_row, lf_row = gr[:h_, :], -_softplus(-gr[h_:, :])
    causal = _tri(l)
    b_col = _sel_dot(causal, lf_col)
    b_row = _dot_sel(lf_row, _tri(l, upper=True))
    scale = dk ** -0.5

    heads = range(h_)
    qs = [q_ref[:, h * dk:(h + 1) * dk] for h in heads]
    ks = [k_ref[:, h * dk:(h + 1) * dk] for h in heads]
    vs = [v_ref[:, h * dv:(h + 1) * dv] for h in heads]
    s_qk = [_dot(qs[h], ks[h], ((1,), (1,))) for h in heads]
    c_prev = [c_ref[cur, h] for h in heads]
    q_c = [_dot(qs[h], c_prev[h]) for h in heads]
    m_prev = [m_ref[cur, h][:, :1] for h in heads]

    m_t, wts, sc = [], [], []
    for h in heads:
        bc, br, ir = b_col[:, h:h + 1], b_row[h:h + 1, :], i_row[h:h + 1, :]
        dmat = jnp.where(causal, bc - br + ir, -jnp.inf)
        inter = bc + m_prev[h]
        m_t.append(jnp.maximum(inter, jnp.max(dmat, axis=1, keepdims=True)))
        wts.append(jnp.exp(dmat - m_t[h]) * scale)
        sc.append(jnp.exp(inter - m_t[h]) * scale)

    qk = [s_qk[h] * wts[h] for h in heads]
    num = [_dot(qk[h], vs[h]) + sc[h] * q_c[h] for h in heads]

    kw, dec, m_new = [], [], []
    for h in heads:
        bc, br = b_col[:, h:h + 1], b_row[h:h + 1, :]
        b_last = bc[l - 1:l, :]
        g_row = b_last - br + i_row[h:h + 1, :]
        g_col = b_last - bc + i_col[:, h:h + 1]
        m_new.append(jnp.maximum(b_last + m_prev[h], jnp.max(g_row, axis=1, keepdims=True)))
        kw.append(ks[h].astype(F32) * jnp.exp(g_col - m_new[h]))
        dec.append(jnp.exp(b_last + m_prev[h] - m_new[h]))
    kv = [_dot(kw[h], vs[h], ((0,), (0,))) for h in heads]

    for h in heads:
        n_prev = n_ref[cur, h]
        den = (jnp.sum(qk[h], axis=1, keepdims=True)
               + sc[h] * jnp.sum(qs[h].astype(F32) * n_prev, axis=1, keepdims=True))
        hh = num[h] / jnp.maximum(jnp.abs(den), jnp.exp(-m_t[h]))
        hn = hh * lax.rsqrt(jnp.mean(hh * hh, axis=1, keepdims=True) + RMS_EPS)
        sl = slice(h * dv, (h + 1) * dv)
        out_ref[:, sl] = (hn * nw_ref[:, sl] * _sigmoid(o_ref[:, sl].astype(F32))
                          ).astype(out_ref.dtype)
        c_ref[nxt, h] = dec[h] * c_prev[h] + kv[h]
        n_ref[nxt, h] = dec[h] * n_prev + jnp.sum(kw[h], axis=0, keepdims=True)
        m_ref[nxt, h] = jnp.broadcast_to(m_new[h], (1, LANES))


def _mlstm_core(proj, gpre, b_gate, norm_w, bsz, seq, l):
    h_, dk, dv = ML_HEADS, ML_DQK, ML_DV
    nc = seq // l
    t = bsz * seq
    row = lambda b, c: (b * nc + c, 0)
    g_rows = _rows_to_chunk_rows(gpre, bsz, nc, l)
    return pl.pallas_call(
        functools.partial(_mlstm_kernel, l=l),
        grid=(bsz, nc),
        in_specs=[pl.BlockSpec((l, h_ * dk), lambda b, c: (b * nc + c, 0)),
                  pl.BlockSpec((l, h_ * dk), lambda b, c: (b * nc + c, 1)),
                  pl.BlockSpec((l, h_ * dv), lambda b, c: (b * nc + c, 1)),
                  pl.BlockSpec((l, h_ * dv), lambda b, c: (b * nc + c, 2)),
                  pl.BlockSpec((l, 2 * h_), row),
                  pl.BlockSpec((None, None, 2 * h_, l), lambda b, c: (b, c, 0, 0)),
                  pl.BlockSpec((1, 2 * h_), lambda b, c: (0, 0)),
                  pl.BlockSpec((2 * h_, 1), lambda b, c: (0, 0)),
                  pl.BlockSpec((1, h_ * dv), lambda b, c: (0, 0))],
        out_specs=pl.BlockSpec((l, h_ * dv), row),
        out_shape=jax.ShapeDtypeStruct((t, h_ * dv), BF16),
        scratch_shapes=[pltpu.VMEM((2, h_, dk, dv), F32),
                        pltpu.VMEM((2, h_, 1, dk), F32),
                        pltpu.VMEM((2, h_, 1, LANES), F32)],
        compiler_params=_params(("parallel", "arbitrary")),
        name="mlstm_core",
    )(proj, proj, proj, proj, gpre, g_rows, b_gate.reshape(1, 2 * h_),
      b_gate.reshape(2 * h_, 1), norm_w.reshape(1, h_ * dv))


def _mlstm_layer(x, res, w_in, b_gate, norm_w, w_out, ln_g, ln_b, bsz, seq):
    proj, gpre = _in_proj(x, w_in, ML_QKVO, 512, 1024, "mlstm_in")
    hn = _mlstm_core(proj, gpre[:, :2 * ML_HEADS], b_gate, norm_w, bsz, seq, ML_CHUNK)
    return _out_ln(hn, w_out.astype(BF16), res, ln_g, ln_b, 512, "mlstm_out_ln")


def _ssd_kernel(z_ref, x_ref, bm_ref, cm_ref, dtc_ref, dtr_ref,
                cwx_ref, cwb_ref, cwc_ref, cbx_ref, cbb_ref, cbc_ref,
                dbc_ref, dbr_ref, alc_ref, alr_ref, dsk_ref, nw_ref,
                out_ref, st_ref, px_ref, pb_ref, pc_ref, *, l):
    g_, hpg, p, n, gw = SSM_GROUPS, SSM_HPG, SSM_HEADDIM, SSM_STATE, SSM_GW
    assert l == LANES and 2 * p == LANES
    tail = SSM_TAIL

    cur = pl.program_id(1) % 2
    nxt = 1 - cur

    @pl.when(pl.program_id(1) == 0)
    def _():
        st_ref[0] = jnp.zeros(st_ref.shape[1:], F32)
        px_ref[0] = jnp.zeros(px_ref.shape[1:], BF16)
        pb_ref[0] = jnp.zeros(pb_ref.shape[1:], BF16)
        pc_ref[0] = jnp.zeros(pc_ref.shape[1:], BF16)

    taps = SSM_CONV - 1
    sel_r = lax.broadcasted_iota(jnp.int32, (taps * l, l + tail), 0)
    sel_c = lax.broadcasted_iota(jnp.int32, (taps * l, l + tail), 1)
    want_c = sel_r + (tail - taps)
    for j in range(1, taps):
        want_c = jnp.where(sel_r >= j * l, sel_r + (tail - taps + j - j * l), want_c)
    shift_sel = jnp.where(sel_c == want_c, 1.0, 0.0).astype(BF16)

    def conv_silu(tail_ref, cur_ref, w_ref, b_ref):
        now = cur_ref[...]
        sh = lax.dot_general(shift_sel, jnp.concatenate([tail_ref[cur], now], axis=0),
                             (((1,), (0,)), ((), ())), preferred_element_type=F32)
        tail_ref[nxt] = now[l - tail:, :]
        acc = b_ref[...] + w_ref[taps:taps + 1, :] * now.astype(F32)
        for j in range(taps):
            acc = acc + w_ref[j:j + 1, :] * sh[j * l:(j + 1) * l]
        return _silu(acc)

    dt_c = _softplus(dtc_ref[...] + dbc_ref[...])
    dt_r = _softplus(dtr_ref[...] + dbr_ref[...])
    causal = _tri(l)
    cum_c = _sel_dot(causal, dt_c * -jnp.exp(alc_ref[...]))
    cum_r = _dot_sel(dt_r * -jnp.exp(alr_ref[...]), _tri(l, upper=True))
    lo = lax.broadcasted_iota(jnp.int32, (l, LANES), 1) < p
    splat = lambda a, c: jnp.broadcast_to(a[:, c:c + 1], (a.shape[0], LANES))

    b_all = conv_silu(pb_ref, bm_ref, cwb_ref, cbb_ref).astype(BF16)
    c_all = conv_silu(pc_ref, cm_ref, cwc_ref, cbc_ref).astype(BF16)
    x_all = conv_silu(px_ref, x_ref, cwx_ref, cbx_ref)

    for g in range(g_):
        ncols = slice(g * n, (g + 1) * n)
        bg, cg = b_all[:, ncols], c_all[:, ncols]
        cb = _dot(cg, bg, ((1,), (1,)))
        st = st_ref[cur, g]
        y_inter = _dot(cg, st)

        xs, ys, xw, dec = [], [], [], []
        for pr in range(hpg // 2):
            he, ho = g * hpg + 2 * pr, g * hpg + 2 * pr + 1
            cols = slice(g * gw + pr * LANES, g * gw + (pr + 1) * LANES)
            x_pair = x_all[:, cols]
            cc_e, cc_o = splat(cum_c, he), splat(cum_c, ho)
            mm_e = (cb * jnp.exp(jnp.where(causal, cc_e - cum_r[he:he + 1, :], -jnp.inf))
                    * dt_r[he:he + 1, :])
            mm_o = (cb * jnp.exp(jnp.where(causal, cc_o - cum_r[ho:ho + 1, :], -jnp.inf))
                    * dt_r[ho:ho + 1, :])
            yy = _dot(jnp.concatenate([mm_e, mm_o], axis=0), x_pair)
            cc = jnp.where(lo, cc_e, cc_o)
            c_last = cc[l - 1:l, :]
            w_s = jnp.exp(c_last - cc) * jnp.where(lo, splat(dt_c, he), splat(dt_c, ho))
            xs.append(x_pair)
            xw.append((x_pair * w_s).astype(BF16))
            dec.append(jnp.exp(c_last))
            ys.append(jnp.where(lo, yy[:l], yy[l:])
                      + y_inter[:, pr * LANES:(pr + 1) * LANES] * jnp.exp(cc))

        gcols = slice(g * gw, (g + 1) * gw)
        st_ref[nxt, g] = (st * jnp.concatenate(dec, axis=1)
                     + _dot(bg, jnp.concatenate(xw, axis=1), ((0,), (0,))))
        y = jnp.concatenate(ys, axis=1) + jnp.concatenate(xs, axis=1) * dsk_ref[:, gcols]
        y = y * _silu(z_ref[:, gcols].astype(F32))
        y = y * lax.rsqrt(jnp.mean(y * y, axis=1, keepdims=True) + RMS_EPS)
        out_ref[:, gcols] = (y * nw_ref[:, gcols]).astype(out_ref.dtype)


def _ssd_core(proj, dtpre, conv_w, conv_b, dt_bias, a_log, d_skip, norm_w, bsz, seq, l):
    g_, p, n, h_ = SSM_GROUPS, SSM_HEADDIM, SSM_STATE, SSM_HEADS
    di, gn = SSM_DINNER, SSM_GROUPS * SSM_STATE
    nc = seq // l
    t = bsz * seq
    dt_rows = _rows_to_chunk_rows(dtpre, bsz, nc, l)
    d_cols = jnp.repeat(d_skip, p).reshape(1, di)
    cwx, cwb, cwc = conv_w[:, :di], conv_w[:, di:di + gn], conv_w[:, di + gn:]
    cb2 = conv_b.reshape(1, -1)
    cbx, cbb, cbc = cb2[:, :di], cb2[:, di:di + gn], cb2[:, di + gn:]
    row = lambda b, c: b * nc + c
    full = lambda a: pl.BlockSpec(a.shape, lambda b, c: (0,) * a.ndim)
    consts = [cwx, cwb, cwc, cbx, cbb, cbc, dt_bias.reshape(1, h_), dt_bias.reshape(h_, 1),
              a_log.reshape(1, h_), a_log.reshape(h_, 1), d_cols, norm_w.reshape(1, di)]
    return pl.pallas_call(
        functools.partial(_ssd_kernel, l=l),
        grid=(bsz, nc),
        in_specs=[pl.BlockSpec((l, di), lambda b, c: (row(b, c), 0)),
                  pl.BlockSpec((l, di), lambda b, c: (row(b, c), 1)),
                  pl.BlockSpec((l, gn), lambda b, c: (row(b, c), 2 * di // gn)),
                  pl.BlockSpec((l, gn), lambda b, c: (row(b, c), 2 * di // gn + 1)),
                  pl.BlockSpec((l, h_), lambda b, c: (row(b, c), 0)),
                  pl.BlockSpec((None, None, h_, l), lambda b, c: (b, c, 0, 0))]
                 + [full(a) for a in consts],
        out_specs=pl.BlockSpec((l, di), lambda b, c: (row(b, c), 0)),
        out_shape=jax.ShapeDtypeStruct((t, di), BF16),
        scratch_shapes=[pltpu.VMEM((2, g_, n, SSM_GW), F32),
                        pltpu.VMEM((2, SSM_TAIL, di), BF16),
                        pltpu.VMEM((2, SSM_TAIL, gn), BF16),
                        pltpu.VMEM((2, SSM_TAIL, gn), BF16)],
        compiler_params=_params(("parallel", "arbitrary")),
        name="ssd_core",
    )(proj, proj, proj, proj, dtpre, dt_rows, *consts)


def _mamba_layer(x, res, w_in, conv_w, conv_b, dt_bias, a_log, d_skip, norm_w, w_out,
                 ln_g, ln_b, bsz, seq):
    proj, dtpre = _in_proj(x, w_in, SSM_ZX, 512, 1024, "ssm_in")
    y = _ssd_core(proj, dtpre[:, :SSM_HEADS], conv_w, conv_b, dt_bias, a_log, d_skip,
                  norm_w, bsz, seq, SSM_CHUNK)
    return _out_ln(y, w_out.astype(BF16), res, ln_g, ln_b, 512, "ssm_out_ln")


def _rwkv_prep_kernel(x_ref, xl_ref, mix_ref, wrkv_ref, w0_ref, w1_ref, w2_ref,
                      a0_ref, a1_ref, a2_ref, g1_ref, g2_ref,
                      r_ref, k_ref, v_ref, lw_ref, a_ref, g_ref, *, tiles_per_seq):
    x = x_ref[...]
    first = pl.program_id(0) % tiles_per_seq == 0
    before = jnp.where(first, 0.0, xl_ref[SUBLANES - 1:SUBLANES, :])
    row = lax.broadcasted_iota(jnp.int32, x.shape, 0)
    xx = jnp.where(row == 0, before, pltpu.roll(x, 1, axis=0)) - x
    mixed = lambda j: x + xx * mix_ref[j:j + 1, :]
    r_ref[...] = _dot(mixed(0), wrkv_ref[0]).astype(r_ref.dtype)
    k_ref[...] = _dot(mixed(2), wrkv_ref[1]).astype(k_ref.dtype)
    v_ref[...] = _dot(mixed(3), wrkv_ref[2]).astype(v_ref.dtype)
    w_raw = w0_ref[...] + _dot(jnp.tanh(_dot(mixed(1), w1_ref[...])), w2_ref[...])
    w = -_softplus(-w_raw) - 0.5
    lw_ref[...] = -jnp.exp(w)
    a_ref[...] = _sigmoid(a0_ref[...] + _dot(_dot(mixed(4), a1_ref[...]), a2_ref[...])
                          ).astype(a_ref.dtype)
    g_ref[...] = _dot(_sigmoid(_dot(mixed(5), g1_ref[...])), g2_ref[...]).astype(g_ref.dtype)


def _rwkv_prep(x, mix, w_rkv, w0, w1, w2, a0, a1, a2, g1, g2, seq, tm):
    t, d = x.shape
    assert seq % tm == 0 and tm % SUBLANES == 0
    rowblk = pl.BlockSpec((tm, d), lambda i: (i, 0))
    prev8 = pl.BlockSpec((SUBLANES, d), lambda i: (jnp.maximum(i * (tm // SUBLANES) - 1, 0), 0))
    ws = [mix, w_rkv.astype(BF16), w0.reshape(1, d), w1.astype(BF16), w2.astype(BF16),
          a0.reshape(1, d), a1.astype(BF16), a2.astype(BF16), g1.astype(BF16),
          g2.astype(BF16)]
    dts = [BF16, BF16, BF16, F32, BF16, BF16]
    return pl.pallas_call(
        functools.partial(_rwkv_prep_kernel, tiles_per_seq=seq // tm),
        grid=(t // tm,),
        in_specs=[rowblk, prev8] + [_resident(a) for a in ws],
        out_specs=[rowblk] * 6,
        out_shape=[jax.ShapeDtypeStruct((t, d), dt) for dt in dts],
        compiler_params=_params(("parallel",)),
        name="rwkv_prep",
    )(x, x, *ws)


def _rwkv_scan_kernel(r_ref, k_ref, v_ref, lw_ref, a_ref, g_ref,
                      kk_ref, ka_ref, rk_ref, lnw_ref, lnb_ref,
                      out_ref, s_ref, *, l):
    hd = RW_HEADDIM
    assert l == hd and 2 * hd == LANES
    n_pairs = r_ref.shape[1] // LANES

    cur = pl.program_id(1) % 2
    nxt = 1 - cur

    @pl.when(pl.program_id(1) == 0)
    def _():
        s_ref[0] = jnp.zeros(s_ref.shape[1:], F32)

    incl = _tri(l)
    lane = lax.broadcasted_iota(jnp.int32, (l, LANES), 1)
    tok_t = lax.broadcasted_iota(jnp.int32, (l, LANES), 0)
    tok_s = lane & (hd - 1)
    lo = lane < hd
    strict2 = tok_s < tok_t
    incl2 = tok_s <= tok_t
    mlo = lambda x: jnp.where(lo, x, 0.0)
    mhi = lambda x: jnp.where(lo, 0.0, x)
    swap = lambda x: pltpu.roll(x, hd, axis=1)
    cat0 = lambda *xs: jnp.concatenate(xs, axis=0)
    nt = ((1,), (1,))
    tn = ((0,), (0,))

    def half_sums(x):
        s_lo = jnp.sum(mlo(x), axis=1, keepdims=True)
        s_hi = jnp.sum(mhi(x), axis=1, keepdims=True)
        return jnp.where(lo, s_lo, s_hi)

    pairs = range(n_pairs)
    cols = [slice(p * LANES, (p + 1) * LANES) for p in pairs]
    steps = l.bit_length() - 1

    st = [dict(lw=lw_ref[:, cols[p]]) for p in pairs]
    for s in st:
        s["gcum"] = _sel_dot(incl, s["lw"])
    for p, s in zip(pairs, st):
        gcum, lw = s["gcum"], s.pop("lw")
        a_sig = a_ref[:, cols[p]].astype(F32)
        k_in = k_ref[:, cols[p]].astype(F32)
        g_last = gcum[l - 1:l, :]
        p_inv = jnp.exp(-gcum)
        p_tail = jnp.exp(g_last - gcum)
        kmod = k_in * (1.0 + (a_sig - 1.0) * ka_ref[:, cols[p]])
        kk = k_in * kk_ref[:, cols[p]]
        kk = kk * lax.rsqrt(jnp.maximum(half_sums(kk * kk), 1e-24))
        b_vec = kk * a_sig
        a_dec = -kk * jnp.exp(gcum - lw)
        r_dec = r_ref[:, cols[p]].astype(F32) * jnp.exp(gcum)
        s.update(a_dec=a_dec, r_dec=r_dec, kmod=kmod, p_last=jnp.exp(g_last),
                 bk_tail=cat0(b_vec * p_tail, kmod * p_tail).astype(BF16),
                 bk_inv=cat0(b_vec * p_inv, kmod * p_inv).astype(BF16))
        del s["gcum"]
    for s in st:
        a_dec, r_dec = s["a_dec"], s["r_dec"]
        s["pm"] = _dot(cat0(mlo(a_dec), mlo(r_dec), mhi(a_dec), mhi(r_dec)), s.pop("bk_inv"), nt)
    for p, s in zip(pairs, st):
        pm = s.pop("pm")
        s["a_e"] = jnp.where(strict2, pm[0 * l:1 * l], 0.0)
        s["a_o"] = jnp.where(strict2, pm[2 * l:3 * l], 0.0)
        s["m_eo"] = cat0(jnp.where(incl2, pm[1 * l:2 * l], 0.0),
                         jnp.where(incl2, pm[3 * l:4 * l], 0.0)).astype(BF16)
        s["v_sw"] = swap(v_ref[:, cols[p]].astype(F32))
    for s in st:
        s["akv"] = _dot(cat0(s["a_e"], s["a_o"]), cat0(jnp.zeros_like(s["v_sw"]), s["v_sw"]))
    for s in st:
        akv, a_dec = s.pop("akv"), s.pop("a_dec")
        s["xx"] = cat0(mlo(a_dec) + mhi(akv[:l]), mhi(a_dec) + mlo(akv[l:]))
        s["n_bd"] = cat0(mlo(s.pop("a_e")), swap(mlo(s.pop("a_o"))))

    for j in range(steps):
        for s in st:
            nb = s["n_bd"].astype(BF16)
            if j + 1 < steps:
                out = _dot(nb, jnp.concatenate([s["xx"].astype(BF16), nb], axis=1))
                s["xx"] = s["xx"] + out[:, :LANES]
                s["n_bd"] = out[:, LANES:]
            else:
                s["xx"] = s["xx"] + _dot(nb, s["xx"])

    for p, s in zip(pairs, st):
        x_e, x_o = s["xx"][:l], s["xx"][l:]
        r_dec = s["r_dec"]
        s["s_pair"] = s_ref[cur, p]
        s["tt"] = _dot(cat0(mlo(x_e), mlo(r_dec), mhi(x_o), mhi(r_dec)),
                       cat0(s["s_pair"], s["s_pair"]), nt)
    for s in st:
        tt = s.pop("tt")
        x_e, x_o = s["xx"][:l], s["xx"][l:]
        uu = mhi(tt[0 * l:1 * l] + x_e) + mlo(tt[2 * l:3 * l] + x_o)
        s["y0"] = mhi(tt[1 * l:2 * l]) + mlo(tt[3 * l:4 * l])
        s["uv"] = cat0(uu, s["v_sw"]).astype(BF16)
    for s in st:
        s["yc"] = _dot(s["m_eo"], s["uv"])
        s["ds"] = _dot(s["uv"], s["bk_tail"], tn)
    for p, s in zip(pairs, st):
        yc, ds = s.pop("yc"), s.pop("ds")
        s["y"] = swap(s["y0"] + mhi(yc[:l]) + mlo(yc[l:]))
        s_ref[nxt, p] = s["s_pair"] * s["p_last"] + jnp.where(lo, ds[l:], ds[:l])

    for p, s in zip(pairs, st):
        y = s["y"]
        inv_hd = 1.0 / hd
        mu = half_sums(y) * inv_hd
        yc = y - mu
        var = half_sums(yc * yc) * inv_hd
        yn = yc * lax.rsqrt(var + RW_LNX_EPS) * lnw_ref[:, cols[p]] + lnb_ref[:, cols[p]]
        bonus = half_sums(r_ref[:, cols[p]].astype(F32) * s["kmod"] * rk_ref[:, cols[p]])
        yn = yn + bonus * v_ref[:, cols[p]].astype(F32)
        out_ref[:, cols[p]] = (yn * g_ref[:, cols[p]].astype(F32)).astype(out_ref.dtype)


def _rwkv_scan(r, k, v, lw, a, g, k_k, k_a, r_k, lnx_w, lnx_b, bsz, seq, l):
    t, d = r.shape
    nc = seq // l
    blk = pl.BlockSpec((l, d), lambda b, c: (b * nc + c, 0))
    vec = pl.BlockSpec((1, d), lambda b, c: (0, 0))
    vecs = [a_.reshape(1, d) for a_ in (k_k, k_a, r_k, lnx_w, lnx_b)]
    return pl.pallas_call(
        functools.partial(_rwkv_scan_kernel, l=l),
        grid=(bsz, nc),
        in_specs=[blk] * 6 + [vec] * 5,
        out_specs=blk,
        out_shape=jax.ShapeDtypeStruct((t, d), BF16),
        scratch_shapes=[pltpu.VMEM((2, d // LANES, RW_HEADDIM, LANES), F32)],
        compiler_params=_params(("parallel", "arbitrary")),
        name="rwkv_scan",
    )(r, k, v, lw, a, g, *vecs)


def _rwkv_layer(x, res, mix, w_rkv, w0, w1, w2, a0, a1, a2, g1, g2, k_k, k_a, r_k,
                lnx_w, lnx_b, w_out, ln_g, ln_b, bsz, seq):
    r, k, v, lw, a, g = _rwkv_prep(x, mix, w_rkv, w0, w1, w2, a0, a1, a2, g1, g2, seq, 512)
    y = _rwkv_scan(r, k, v, lw, a, g, k_k, k_a, r_k, lnx_w, lnx_b, bsz, seq, RW_CHUNK)
    return _out_ln(y, w_out.astype(BF16), res, ln_g, ln_b, 512, "rwkv_out_ln")


def kernel(x, ln_g, ln_b, ffn_w_in, ffn_w_out, ml_w_in, ml_b_gate, ml_norm_w, ml_w_out, ssm_w_in, ssm_conv_w, ssm_conv_b, ssm_dt_bias, ssm_a_log, ssm_d, ssm_norm_w, ssm_w_out, rw_mix, rw_w_rkv, rw_w0, rw_w1, rw_w2, rw_a0, rw_a1, rw_a2, rw_g1, rw_g2, rw_k_k, rw_k_a, rw_r_k, rw_lnx_w, rw_lnx_b, rw_w_out):
    bsz, seq, d = x.shape
    h = x.reshape(bsz * seq, d)
    for i in range(DEPTH):
        kind, j = i % N_MIXERS, i // N_MIXERS
        if kind == 0:
            h = _mlstm_layer(h, h, ml_w_in[j], ml_b_gate[j], ml_norm_w[j], ml_w_out[j],
                             ln_g[i, 0], ln_b[i, 0], bsz, seq)
        elif kind == 1:
            h = _mamba_layer(h, h, ssm_w_in[j], ssm_conv_w[j], ssm_conv_b[j], ssm_dt_bias[j],
                             ssm_a_log[j], ssm_d[j], ssm_norm_w[j], ssm_w_out[j],
                             ln_g[i, 0], ln_b[i, 0], bsz, seq)
        else:
            h = _rwkv_layer(h, h, rw_mix[j], rw_w_rkv[j], rw_w0[j], rw_w1[j], rw_w2[j],
                            rw_a0[j], rw_a1[j], rw_a2[j], rw_g1[j], rw_g2[j], rw_k_k[j],
                            rw_k_a[j], rw_r_k[j].reshape(-1), rw_lnx_w[j], rw_lnx_b[j],
                            rw_w_out[j], ln_g[i, 0], ln_b[i, 0], bsz, seq)
        h = _ffn_ln(h, ffn_w_in[i].astype(BF16), ffn_w_out[i].astype(BF16),
                    ln_g[i, 1], ln_b[i, 1], 512, 1408)
    return h.reshape(bsz, seq, d)
```

```python
import functools
import math

import jax
import jax.numpy as jnp
from jax import lax
from jax.experimental import pallas as pl
from jax.experimental.pallas import tpu as pltpu

F32 = jnp.float32
BF16 = jnp.bfloat16

D_MODEL = 1024
DEPTH = 4
N_MIXERS = 3
DEEPNORM_ALPHA = (2 * DEPTH) ** 0.25
LN_EPS = 1e-5
RMS_EPS = 1e-6

ML_HEADS = 4
ML_DV = D_MODEL // ML_HEADS
ML_DQK = ML_DV // 2
ML_GATE_CAP = 15.0
ML_QKVO = 2 * ML_HEADS * ML_DQK + 2 * ML_HEADS * ML_DV

SSM_DINNER = 2 * D_MODEL
SSM_HEADDIM = 64
SSM_HEADS = SSM_DINNER // SSM_HEADDIM
SSM_STATE = 128
SSM_GROUPS = 4
SSM_CONV = 4
SSM_HPG = SSM_HEADS // SSM_GROUPS
SSM_GW = SSM_DINNER // SSM_GROUPS
SSM_ZX = 2 * SSM_DINNER + 2 * SSM_GROUPS * SSM_STATE

RW_HEADDIM = 64
RW_HEADS = D_MODEL // RW_HEADDIM
RW_LNX_EPS = 64e-5

FFN_HIDDEN = int(math.ceil(8 * D_MODEL / 3 / 256)) * 256

LANES = 128
SUBLANES = 8
VMEM_LIMIT = 56 * 1024 * 1024

ML_CHUNK = 256
SSM_CHUNK = 128
SSM_TAIL = 2 * SUBLANES
RW_CHUNK = 64


def _params(sem):
    return pltpu.CompilerParams(dimension_semantics=sem, vmem_limit_bytes=VMEM_LIMIT)


def _dot(a, b, dims=((1,), (0,))):
    return lax.dot_general(a.astype(BF16), b.astype(BF16), (dims, ((), ())),
                           preferred_element_type=F32)


def _split3(a):
    hi = a.astype(BF16)
    r1 = a - hi.astype(F32)
    mid = r1.astype(BF16)
    lo = (r1 - mid.astype(F32)).astype(BF16)
    return hi, mid, lo


def _dot_sel(a, sel, dims=((1,), (0,))):
    sel = sel.astype(BF16)
    return sum(lax.dot_general(p, sel, (dims, ((), ())), preferred_element_type=F32)
               for p in _split3(a))


def _sel_dot(sel, a, dims=((1,), (0,))):
    sel = sel.astype(BF16)
    return sum(lax.dot_general(sel, p, (dims, ((), ())), preferred_element_type=F32)
               for p in _split3(a))


def _softplus(x):
    return jnp.maximum(x, 0.0) + jnp.log1p(jnp.exp(-jnp.abs(x)))


def _sigmoid(x):
    return 0.5 * jnp.tanh(0.5 * x) + 0.5


def _silu(x):
    return x * _sigmoid(x)


def _layer_norm(x, g, b):
    mu = jnp.mean(x, axis=-1, keepdims=True)
    xc = x - mu
    var = jnp.mean(xc * xc, axis=-1, keepdims=True)
    return xc * lax.rsqrt(var + LN_EPS) * g + b


def _tri(n, strict=False, upper=False):
    r = lax.broadcasted_iota(jnp.int32, (n, n), 0)
    c = lax.broadcasted_iota(jnp.int32, (n, n), 1)
    if upper:
        r, c = c, r
    return (c < r) if strict else (c <= r)


def _resident(a):
    return pl.BlockSpec(a.shape, lambda *_: (0,) * a.ndim, pipeline_mode=pl.Buffered(1))


def _in_proj_kernel(x_ref, w_ref, ws_ref, o_ref, os_ref, *, tn):
    x = x_ref[...]
    xb = x.astype(BF16)
    cols = [slice(j * tn, (j + 1) * tn) for j in range(w_ref.shape[1] // tn)]
    acc = _dot(xb, w_ref[:, cols[0]])
    for j in range(len(cols)):
        nxt = _dot(xb, w_ref[:, cols[j + 1]]) if j + 1 < len(cols) else None
        o_ref[:, cols[j]] = acc.astype(o_ref.dtype)
        acc = nxt
    x_lo = (x - xb.astype(F32)).astype(BF16)
    both = _dot(xb, ws_ref[...])
    os_ref[...] = both[:, :LANES] + both[:, LANES:] + _dot(x_lo, ws_ref[:, :LANES])


def _in_proj(x, w, n_main, tm, tn, name):
    t, k = x.shape
    assert t % tm == 0 and n_main % tn == 0 and w.shape[1] - n_main <= LANES
    w_main = w[:, :n_main].astype(BF16)
    w_few = jnp.pad(w[:, n_main:], ((0, 0), (0, LANES - (w.shape[1] - n_main))))
    w_hi = w_few.astype(BF16)
    w_lo = (w_few - w_hi.astype(F32)).astype(BF16)
    w_split = jnp.concatenate([w_hi, w_lo], axis=1)
    return pl.pallas_call(
        functools.partial(_in_proj_kernel, tn=tn),
        grid=(t // tm,),
        in_specs=[pl.BlockSpec((tm, k), lambda i: (i, 0)), _resident(w_main), _resident(w_split)],
        out_specs=[pl.BlockSpec((tm, n_main), lambda i: (i, 0)),
                   pl.BlockSpec((tm, LANES), lambda i: (i, 0))],
        out_shape=[jax.ShapeDtypeStruct((t, n_main), BF16),
                   jax.ShapeDtypeStruct((t, LANES), F32)],
        compiler_params=_params(("parallel",)),
        name=name,
    )(x, w_main, w_split)


def _out_ln_kernel(y_ref, w_ref, res_ref, g_ref, b_ref, o_ref, *, ts):
    n_sub = y_ref.shape[0] // ts
    rows = [slice(s * ts, (s + 1) * ts) for s in range(n_sub)]
    acc = _dot(y_ref[rows[0], :], w_ref[...])
    for s in range(n_sub):
        nxt = _dot(y_ref[rows[s + 1], :], w_ref[...]) if s + 1 < n_sub else None
        o_ref[rows[s], :] = _layer_norm(DEEPNORM_ALPHA * res_ref[rows[s], :] + acc,
                                        g_ref[...], b_ref[...])
        acc = nxt


def _out_ln(y, w, res, g, b, tm, ts, name):
    t, k = y.shape
    d = w.shape[1]
    assert tm % ts == 0
    g2, b2 = g.reshape(1, d), b.reshape(1, d)
    return pl.pallas_call(
        functools.partial(_out_ln_kernel, ts=ts),
        grid=(t // tm,),
        in_specs=[pl.BlockSpec((tm, k), lambda i: (i, 0)),
                  _resident(w),
                  pl.BlockSpec((tm, d), lambda i: (i, 0)),
                  _resident(g2), _resident(b2)],
        out_specs=pl.BlockSpec((tm, d), lambda i: (i, 0)),
        out_shape=jax.ShapeDtypeStruct((t, d), F32),
        compiler_params=_params(("parallel",)),
        name=name,
    )(y, w, res, g2, b2)


def _ffn_kernel(x_ref, wi_ref, wo_ref, g_ref, b_ref, o_ref, *, tf, ts):
    f = wo_ref.shape[0]
    nf = f // tf

    def finish(rows, x, acc):
        o_ref[rows, :] = _layer_norm(DEEPNORM_ALPHA * x + acc, g_ref[...], b_ref[...])

    pending = None
    for s in range(x_ref.shape[0] // ts):
        rows = slice(s * ts, (s + 1) * ts)
        x = x_ref[rows, :]
        xb = x.astype(BF16)
        gate_up = lambda j: (_dot(xb, wi_ref[:, j * tf:(j + 1) * tf]),
                             _dot(xb, wi_ref[:, f + j * tf:f + (j + 1) * tf]))
        acc = None
        nxt = gate_up(0)
        if pending is not None:
            finish(*pending)
        for j in range(nf):
            gate, up = nxt
            if j + 1 < nf:
                nxt = gate_up(j + 1)
            part = _dot(_silu(gate) * up, wo_ref[j * tf:(j + 1) * tf, :])
            acc = part if acc is None else acc + part
        pending = (rows, x, acc)
    finish(*pending)


def _ffn_ln(x, w_in, w_out, g, b, tm, tf, ts):
    t, d = x.shape
    f = w_out.shape[0]
    assert f % tf == 0 and tf % LANES == 0 and tm % ts == 0
    g2, b2 = g.reshape(1, d), b.reshape(1, d)
    return pl.pallas_call(
        functools.partial(_ffn_kernel, tf=tf, ts=ts),
        grid=(t // tm,),
        in_specs=[pl.BlockSpec((tm, d), lambda i: (i, 0)),
                  _resident(w_in), _resident(w_out), _resident(g2), _resident(b2)],
        out_specs=pl.BlockSpec((tm, d), lambda i: (i, 0)),
        out_shape=jax.ShapeDtypeStruct((t, d), F32),
        compiler_params=_params(("parallel",)),
        name="ffn_ln",
    )(x, w_in, w_out, g2, b2)


def _rows_to_chunk_rows(a, bsz, nc, l):
    c = a.shape[1]
    return a.reshape(bsz, nc, l, c).transpose(0, 1, 3, 2)


def _mlstm_kernel(q_ref, k_ref, v_ref, o_ref, gc_ref, gr_ref, bc_ref, br_ref, nw_ref,
                  out_ref, c_ref, n_ref, m_ref, *, l):
    h_, dk, dv = ML_HEADS, ML_DQK, ML_DV

    cur = pl.program_id(1) % 2
    nxt = 1 - cur

    @pl.when(pl.program_id(1) == 0)
    def _():
        c_ref[0] = jnp.zeros(c_ref.shape[1:], F32)
        n_ref[0] = jnp.zeros(n_ref.shape[1:], F32)
        m_ref[0] = jnp.zeros(m_ref.shape[1:], F32)

    cap = lambda t: ML_GATE_CAP * jnp.tanh(t / ML_GATE_CAP)
    gc = cap(gc_ref[...] + bc_ref[...])
    gr = cap(gr_ref[...] + br_ref[...])
    i_col, lf_col = gc[:, :h_], -_softplus(-gc[:, h_:])
    i_row, lf_row = gr[:h_, :], -_softplus(-gr[h_:, :])
    causal = _tri(l)
    b_col = _sel_dot(causal, lf_col)
    b_row = _dot_sel(lf_row, _tri(l, upper=True))
    scale = dk ** -0.5

    heads = range(h_)
    qs = [q_ref[:, h * dk:(h + 1) * dk] for h in heads]
    ks = [k_ref[:, h * dk:(h + 1) * dk] for h in heads]
    vs = [v_ref[:, h * dv:(h + 1) * dv] for h in heads]
    s_qk = [_dot(qs[h], ks[h], ((1,), (1,))) for h in heads]
    c_prev = [c_ref[cur, h] for h in heads]
    q_c = [_dot(qs[h], c_prev[h]) for h in heads]
    m_prev = [m_ref[cur, h][:, :1] for h in heads]

    m_t, wts, sc = [], [], []
    for h in heads:
        bc, br, ir = b_col[:, h:h + 1], b_row[h:h + 1, :], i_row[h:h + 1, :]
        dmat = jnp.where(causal, bc - br + ir, -jnp.inf)
        inter = bc + m_prev[h]
        m_t.append(jnp.maximum(inter, jnp.max(dmat, axis=1, keepdims=True)))
        wts.append(jnp.exp(dmat - m_t[h]) * scale)
        sc.append(jnp.exp(inter - m_t[h]) * scale)

    qk = [s_qk[h] * wts[h] for h in heads]
    num = [_dot(qk[h], vs[h]) + sc[h] * q_c[h] for h in heads]

    kw, dec, m_new = [], [], []
    for h in heads:
        bc, br = b_col[:, h:h + 1], b_row[h:h + 1, :]
        b_last = bc[l - 1:l, :]
        g_row = b_last - br + i_row[h:h + 1, :]
        g_col = b_last - bc + i_col[:, h:h + 1]
        m_new.append(jnp.maximum(b_last + m_prev[h], jnp.max(g_row, axis=1, keepdims=True)))
        kw.append(ks[h].astype(F32) * jnp.exp(g_col - m_new[h]))
        dec.append(jnp.exp(b_last + m_prev[h] - m_new[h]))
    kv = [_dot(kw[h], vs[h], ((0,), (0,))) for h in heads]

    for h in heads:
        n_prev = n_ref[cur, h]
        den = (jnp.sum(qk[h], axis=1, keepdims=True)
               + sc[h] * jnp.sum(qs[h].astype(F32) * n_prev, axis=1, keepdims=True))
        hh = num[h] / jnp.maximum(jnp.abs(den), jnp.exp(-m_t[h]))
        hn = hh * lax.rsqrt(jnp.mean(hh * hh, axis=1, keepdims=True) + RMS_EPS)
        sl = slice(h * dv, (h + 1) * dv)
        out_ref[:, sl] = (hn * nw_ref[:, sl] * _sigmoid(o_ref[:, sl].astype(F32))
                          ).astype(out_ref.dtype)
        c_ref[nxt, h] = dec[h] * c_prev[h] + kv[h]
        n_ref[nxt, h] = dec[h] * n_prev + jnp.sum(kw[h], axis=0, keepdims=True)
        m_ref[nxt, h] = jnp.broadcast_to(m_new[h], (1, LANES))


def _mlstm_core(proj, gpre, b_gate, norm_w, bsz, seq, l):
    h_, dk, dv = ML_HEADS, ML_DQK, ML_DV
    nc = seq // l
    t = bsz * seq
    row = lambda b, c: (b * nc + c, 0)
    g_rows = _rows_to_chunk_rows(gpre, bsz, nc, l)
    return pl.pallas_call(
        functools.partial(_mlstm_kernel, l=l),
        grid=(bsz, nc),
        in_specs=[pl.BlockSpec((l, h_ * dk), lambda b, c: (b * nc + c, 0)),
                  pl.BlockSpec((l, h_ * dk), lambda b, c: (b * nc + c, 1)),
                  pl.BlockSpec((l, h_ * dv), lambda b, c: (b * nc + c, 1)),
                  pl.BlockSpec((l, h_ * dv), lambda b, c: (b * nc + c, 2)),
                  pl.BlockSpec((l, 2 * h_), row),
                  pl.BlockSpec((None, None, 2 * h_, l), lambda b, c: (b, c, 0, 0)),
                  pl.BlockSpec((1, 2 * h_), lambda b, c: (0, 0)),
                  pl.BlockSpec((2 * h_, 1), lambda b, c: (0, 0)),
                  pl.BlockSpec((1, h_ * dv), lambda b, c: (0, 0))],
        out_specs=pl.BlockSpec((l, h_ * dv), row),
        out_shape=jax.ShapeDtypeStruct((t, h_ * dv), BF16),
        scratch_shapes=[pltpu.VMEM((2, h_, dk, dv), F32),
                        pltpu.VMEM((2, h_, 1, dk), F32),
                        pltpu.VMEM((2, h_, 1, LANES), F32)],
        compiler_params=_params(("parallel", "arbitrary")),
        name="mlstm_core",
    )(proj, proj, proj, proj, gpre, g_rows, b_gate.reshape(1, 2 * h_),
      b_gate.reshape(2 * h_, 1), norm_w.reshape(1, h_ * dv))


def _mlstm_layer(x, res, w_in, b_gate, norm_w, w_out, ln_g, ln_b, bsz, seq):
    proj, gpre = _in_proj(x, w_in, ML_QKVO, 512, 1024, "mlstm_in")
    hn = _mlstm_core(proj, gpre[:, :2 * ML_HEADS], b_gate, norm_w, bsz, seq, ML_CHUNK)
    return _out_ln(hn, w_out.astype(BF16), res, ln_g, ln_b, 1024, 256, "mlstm_out_ln")


def _ssd_kernel(z_ref, x_ref, bm_ref, cm_ref, dtc_ref, dtr_ref,
                cwx_ref, cwb_ref, cwc_ref, cbx_ref, cbb_ref, cbc_ref,
                dbc_ref, dbr_ref, alc_ref, alr_ref, dsk_ref, nw_ref,
                out_ref, st_ref, px_ref, pb_ref, pc_ref, *, l):
    g_, hpg, p, n, gw = SSM_GROUPS, SSM_HPG, SSM_HEADDIM, SSM_STATE, SSM_GW
    assert l == LANES and 2 * p == LANES
    tail = SSM_TAIL

    cur = pl.program_id(1) % 2
    nxt = 1 - cur

    @pl.when(pl.program_id(1) == 0)
    def _():
        st_ref[0] = jnp.zeros(st_ref.shape[1:], F32)
        px_ref[0] = jnp.zeros(px_ref.shape[1:], BF16)
        pb_ref[0] = jnp.zeros(pb_ref.shape[1:], BF16)
        pc_ref[0] = jnp.zeros(pc_ref.shape[1:], BF16)

    taps = SSM_CONV - 1
    sel_r = lax.broadcasted_iota(jnp.int32, (taps * l, l + tail), 0)
    sel_c = lax.broadcasted_iota(jnp.int32, (taps * l, l + tail), 1)
    want_c = sel_r + (tail - taps)
    for j in range(1, taps):
        want_c = jnp.where(sel_r >= j * l, sel_r + (tail - taps + j - j * l), want_c)
    shift_sel = jnp.where(sel_c == want_c, 1.0, 0.0).astype(BF16)

    def conv_silu(tail_ref, cur_ref, w_ref, b_ref):
        now = cur_ref[...]
        sh = lax.dot_general(shift_sel, jnp.concatenate([tail_ref[cur], now], axis=0),
                             (((1,), (0,)), ((), ())), preferred_element_type=F32)
        tail_ref[nxt] = now[l - tail:, :]
        acc = b_ref[...] + w_ref[taps:taps + 1, :] * now.astype(F32)
        for j in range(taps):
            acc = acc + w_ref[j:j + 1, :] * sh[j * l:(j + 1) * l]
        return _silu(acc)

    dt_c = _softplus(dtc_ref[...] + dbc_ref[...])
    dt_r = _softplus(dtr_ref[...] + dbr_ref[...])
    causal = _tri(l)
    cum_c = _sel_dot(causal, dt_c * -jnp.exp(alc_ref[...]))
    cum_r = _dot_sel(dt_r * -jnp.exp(alr_ref[...]), _tri(l, upper=True))
    lo = lax.broadcasted_iota(jnp.int32, (l, LANES), 1) < p
    splat = lambda a, c: jnp.broadcast_to(a[:, c:c + 1], (a.shape[0], LANES))

    b_all = conv_silu(pb_ref, bm_ref, cwb_ref, cbb_ref).astype(BF16)
    c_all = conv_silu(pc_ref, cm_ref, cwc_ref, cbc_ref).astype(BF16)
    x_all = conv_silu(px_ref, x_ref, cwx_ref, cbx_ref)

    for g in range(g_):
        ncols = slice(g * n, (g + 1) * n)
        bg, cg = b_all[:, ncols], c_all[:, ncols]
        cb = _dot(cg, bg, ((1,), (1,)))
        st = st_ref[cur, g]
        y_inter = _dot(cg, st)

        xs, ys, xw, dec = [], [], [], []
        for pr in range(hpg // 2):
            he, ho = g * hpg + 2 * pr, g * hpg + 2 * pr + 1
            cols = slice(g * gw + pr * LANES, g * gw + (pr + 1) * LANES)
            x_pair = x_all[:, cols]
            cc_e, cc_o = splat(cum_c, he), splat(cum_c, ho)
            mm_e = (cb * jnp.exp(jnp.where(causal, cc_e - cum_r[he:he + 1, :], -jnp.inf))
                    * dt_r[he:he + 1, :])
            mm_o = (cb * jnp.exp(jnp.where(causal, cc_o - cum_r[ho:ho + 1, :], -jnp.inf))
                    * dt_r[ho:ho + 1, :])
            yy = _dot(jnp.concatenate([mm_e, mm_o], axis=0), x_pair)
            cc = jnp.where(lo, cc_e, cc_o)
            c_last = cc[l - 1:l, :]
            w_s = jnp.exp(c_last - cc) * jnp.where(lo, splat(dt_c, he), splat(dt_c, ho))
            xs.append(x_pair)
            xw.append((x_pair * w_s).astype(BF16))
            dec.append(jnp.exp(c_last))
            ys.append(jnp.where(lo, yy[:l], yy[l:])
                      + y_inter[:, pr * LANES:(pr + 1) * LANES] * jnp.exp(cc))

        gcols = slice(g * gw, (g + 1) * gw)
        st_ref[nxt, g] = (st * jnp.concatenate(dec, axis=1)
                     + _dot(bg, jnp.concatenate(xw, axis=1), ((0,), (0,))))
        y = jnp.concatenate(ys, axis=1) + jnp.concatenate(xs, axis=1) * dsk_ref[:, gcols]
        y = y * _silu(z_ref[:, gcols].astype(F32))
        y = y * lax.rsqrt(jnp.mean(y * y, axis=1, keepdims=True) + RMS_EPS)
        out_ref[:, gcols] = (y * nw_ref[:, gcols]).astype(out_ref.dtype)


def _ssd_core(proj, dtpre, conv_w, conv_b, dt_bias, a_log, d_skip, norm_w, bsz, seq, l):
    g_, p, n, h_ = SSM_GROUPS, SSM_HEADDIM, SSM_STATE, SSM_HEADS
    di, gn = SSM_DINNER, SSM_GROUPS * SSM_STATE
    nc = seq // l
    t = bsz * seq
    dt_rows = _rows_to_chunk_rows(dtpre, bsz, nc, l)
    d_cols = jnp.repeat(d_skip, p).reshape(1, di)
    cwx, cwb, cwc = conv_w[:, :di], conv_w[:, di:di + gn], conv_w[:, di + gn:]
    cb2 = conv_b.reshape(1, -1)
    cbx, cbb, cbc = cb2[:, :di], cb2[:, di:di + gn], cb2[:, di + gn:]
    row = lambda b, c: b * nc + c
    full = lambda a: pl.BlockSpec(a.shape, lambda b, c: (0,) * a.ndim)
    consts = [cwx, cwb, cwc, cbx, cbb, cbc, dt_bias.reshape(1, h_), dt_bias.reshape(h_, 1),
              a_log.reshape(1, h_), a_log.reshape(h_, 1), d_cols, norm_w.reshape(1, di)]
    return pl.pallas_call(
        functools.partial(_ssd_kernel, l=l),
        grid=(bsz, nc),
        in_specs=[pl.BlockSpec((l, di), lambda b, c: (row(b, c), 0)),
                  pl.BlockSpec((l, di), lambda b, c: (row(b, c), 1)),
                  pl.BlockSpec((l, gn), lambda b, c: (row(b, c), 2 * di // gn)),
                  pl.BlockSpec((l, gn), lambda b, c: (row(b, c), 2 * di // gn + 1)),
                  pl.BlockSpec((l, h_), lambda b, c: (row(b, c), 0)),
                  pl.BlockSpec((None, None, h_, l), lambda b, c: (b, c, 0, 0))]
                 + [full(a) for a in consts],
        out_specs=pl.BlockSpec((l, di), lambda b, c: (row(b, c), 0)),
        out_shape=jax.ShapeDtypeStruct((t, di), BF16),
        scratch_shapes=[pltpu.VMEM((2, g_, n, SSM_GW), F32),
                        pltpu.VMEM((2, SSM_TAIL, di), BF16),
                        pltpu.VMEM((2, SSM_TAIL, gn), BF16),
                        pltpu.VMEM((2, SSM_TAIL, gn), BF16)],
        compiler_params=_params(("parallel", "arbitrary")),
        name="ssd_core",
    )(proj, proj, proj, proj, dtpre, dt_rows, *consts)


def _mamba_layer(x, res, w_in, conv_w, conv_b, dt_bias, a_log, d_skip, norm_w, w_out,
                 ln_g, ln_b, bsz, seq):
    proj, dtpre = _in_proj(x, w_in, SSM_ZX, 512, 1024, "ssm_in")
    y = _ssd_core(proj, dtpre[:, :SSM_HEADS], conv_w, conv_b, dt_bias, a_log, d_skip,
                  norm_w, bsz, seq, SSM_CHUNK)
    return _out_ln(y, w_out.astype(BF16), res, ln_g, ln_b, 1024, 256, "ssm_out_ln")


def _rwkv_prep_kernel(x_ref, xl_ref, mix_ref, wrkv_ref, w0_ref, w1_ref, w2_ref,
                      a0_ref, a1_ref, a2_ref, g1_ref, g2_ref,
                      r_ref, k_ref, v_ref, lw_ref, a_ref, g_ref, *, tiles_per_seq):
    x = x_ref[...]
    first = pl.program_id(0) % tiles_per_seq == 0
    before = jnp.where(first, 0.0, xl_ref[SUBLANES - 1:SUBLANES, :])
    row = lax.broadcasted_iota(jnp.int32, x.shape, 0)
    xx = jnp.where(row == 0, before, pltpu.roll(x, 1, axis=0)) - x
    mixed = lambda j: x + xx * mix_ref[j:j + 1, :]
    r_ref[...] = _dot(mixed(0), wrkv_ref[0]).astype(r_ref.dtype)
    k_ref[...] = _dot(mixed(2), wrkv_ref[1]).astype(k_ref.dtype)
    v_ref[...] = _dot(mixed(3), wrkv_ref[2]).astype(v_ref.dtype)
    w_raw = w0_ref[...] + _dot(jnp.tanh(_dot(mixed(1), w1_ref[...])), w2_ref[...])
    w = -_softplus(-w_raw) - 0.5
    lw_ref[...] = -jnp.exp(w)
    a_ref[...] = _sigmoid(a0_ref[...] + _dot(_dot(mixed(4), a1_ref[...]), a2_ref[...])
                          ).astype(a_ref.dtype)
    g_ref[...] = _dot(_sigmoid(_dot(mixed(5), g1_ref[...])), g2_ref[...]).astype(g_ref.dtype)


def _rwkv_prep(x, mix, w_rkv, w0, w1, w2, a0, a1, a2, g1, g2, seq, tm):
    t, d = x.shape
    assert seq % tm == 0 and tm % SUBLANES == 0
    rowblk = pl.BlockSpec((tm, d), lambda i: (i, 0))
    prev8 = pl.BlockSpec((SUBLANES, d), lambda i: (jnp.maximum(i * (tm // SUBLANES) - 1, 0), 0))
    ws = [mix, w_rkv.astype(BF16), w0.reshape(1, d), w1.astype(BF16), w2.astype(BF16),
          a0.reshape(1, d), a1.astype(BF16), a2.astype(BF16), g1.astype(BF16),
          g2.astype(BF16)]
    dts = [BF16, BF16, BF16, F32, BF16, BF16]
    return pl.pallas_call(
        functools.partial(_rwkv_prep_kernel, tiles_per_seq=seq // tm),
        grid=(t // tm,),
        in_specs=[rowblk, prev8] + [_resident(a) for a in ws],
        out_specs=[rowblk] * 6,
        out_shape=[jax.ShapeDtypeStruct((t, d), dt) for dt in dts],
        compiler_params=_params(("parallel",)),
        name="rwkv_prep",
    )(x, x, *ws)


def _rwkv_scan_kernel(r_ref, k_ref, v_ref, lw_ref, a_ref, g_ref,
                      kk_ref, ka_ref, rk_ref, lnw_ref, lnb_ref,
                      out_ref, s_ref, *, l):
    hd = RW_HEADDIM
    assert l == hd and 2 * hd == LANES
    n_pairs = r_ref.shape[1] // LANES

    cur = pl.program_id(1) % 2
    nxt = 1 - cur

    @pl.when(pl.program_id(1) == 0)
    def _():
        s_ref[0] = jnp.zeros(s_ref.shape[1:], F32)

    lane =lax.broadcasted_iota(jnp.int32, (l, LANES), 1)
    tok_t = lax.broadcasted_iota(jnp.int32, (l, LANES), 0)
    tok_s = lane & (hd - 1)
    lo = lane < hd
    strict2 = tok_s < tok_t
    incl2 = tok_s <= tok_t
    mlo = lambda x: jnp.where(lo, x, 0.0)
    mhi = lambda x: jnp.where(lo, 0.0, x)
    swap = lambda x: pltpu.roll(x, hd, axis=1)
    cat0 = lambda *xs: jnp.concatenate(xs, axis=0)
    nt = ((1,), (1,))
    tn = ((0,), (0,))

    def half_sums(x):
        s_lo = jnp.sum(mlo(x), axis=1, keepdims=True)
        s_hi = jnp.sum(mhi(x), axis=1, keepdims=True)
        return jnp.where(lo, s_lo, s_hi)

    steps = l.bit_length() - 1
    n_sub = r_ref.shape[0] // l

    st = [dict(p=p, rows=slice(c * l, (c + 1) * l), cols=slice(p * LANES, (p + 1) * LANES))
          for c in range(n_sub) for p in range(n_pairs)]
    for s in st:
        s["lw"] = lw_ref[s["rows"], s["cols"]]
    for s in st:
        gcum = s["lw"]
        for k in range(steps):
            gcum = gcum + jnp.where(tok_t >= 2 ** k, pltpu.roll(gcum, 2 ** k, axis=0), 0.0)
        s["gcum"] = gcum
    for s in st:
        rows, cols = s["rows"], s["cols"]
        gcum, lw = s["gcum"], s.pop("lw")
        a_sig = a_ref[rows, cols].astype(F32)
        k_in = k_ref[rows, cols].astype(F32)
        g_last = gcum[l - 1:l, :]
        p_inv = jnp.exp(-gcum)
        p_tail = jnp.exp(g_last - gcum)
        kmod = k_in * (1.0 + (a_sig - 1.0) * ka_ref[:, cols])
        kk = k_in * kk_ref[:, cols]
        kk = kk * lax.rsqrt(jnp.maximum(half_sums(kk * kk), 1e-24))
        b_vec = kk * a_sig
        a_dec = -kk * jnp.exp(gcum - lw)
        r_dec = r_ref[rows, cols].astype(F32) * jnp.exp(gcum)
        b_inv, k_inv = b_vec * p_inv, kmod * p_inv
        s.update(a_dec=a_dec, r_dec=r_dec, kmod=kmod, p_last=jnp.exp(g_last),
                 bk_tail=cat0(b_vec * p_tail, kmod * p_tail).astype(BF16),
                 bk_inv=cat0(mlo(b_inv), mlo(k_inv), mhi(b_inv), mhi(k_inv)).astype(BF16))
        del s["gcum"]
    for s in st:
        s["pm"] = _dot(cat0(s["a_dec"], s["r_dec"]), s.pop("bk_inv"), nt)
    for s in st:
        pm = s.pop("pm")
        s["a_e"] = jnp.where(strict2, pm[:l, :LANES], 0.0)
        s["a_o"] = jnp.where(strict2, pm[:l, LANES:], 0.0)
        s["m_eo"] = cat0(jnp.where(incl2, pm[l:, :LANES], 0.0),
                         jnp.where(incl2, pm[l:, LANES:], 0.0)).astype(BF16)
        s["v"] = v_ref[s["rows"], s["cols"]].astype(F32)
        s["v_sw"] = swap(s["v"])
    for s in st:
        s["akv"] = _dot(cat0(s["a_e"], s["a_o"]), cat0(jnp.zeros_like(s["v_sw"]), s["v_sw"]))
    for s in st:
        akv, a_dec = s.pop("akv"), s.pop("a_dec")
        s["xx"] = cat0(mlo(a_dec) + mhi(akv[:l]), mhi(a_dec) + mlo(akv[l:]))
        s["n_bd"] = cat0(mlo(s.pop("a_e")), swap(mlo(s.pop("a_o"))))

    for j in range(steps):
        for s in st:
            nb = s["n_bd"].astype(BF16)
            if j + 1 < steps:
                out = _dot(nb, jnp.concatenate([s["xx"].astype(BF16), nb], axis=1))
                s["xx"] = s["xx"] + out[:, :LANES]
                s["n_bd"] = out[:, LANES:]
            else:
                s["xx"] = s["xx"] + _dot(nb, s["xx"])

    state = [s_ref[cur, p] for p in range(n_pairs)]
    for c in range(n_sub):
        sub = st[c * n_pairs:(c + 1) * n_pairs]
        for s in sub:
            x_e, x_o = s["xx"][:l], s["xx"][l:]
            s_pair = state[s["p"]]
            s["tt"] = _dot(cat0(mlo(x_e) + mhi(x_o), s["r_dec"]),
                           cat0(mlo(s_pair), mhi(s_pair)), nt)
        for s in sub:
            tt = s.pop("tt")
            x_e, x_o = s["xx"][:l], s["xx"][l:]
            uu = tt[:l] + swap(mhi(x_e) + mlo(x_o))
            s["y0"] = tt[l:]
            s["uv"] = cat0(uu, s["v"]).astype(BF16)
        for s in sub:
            s["yc"] = _dot(s["m_eo"], s["uv"])
            s["ds"] = _dot(s["uv"], s["bk_tail"], tn)
        for s in sub:
            yc, ds = s.pop("yc"), s.pop("ds")
            s["y"] = s["y0"] + mlo(yc[:l]) + mhi(yc[l:])
            state[s["p"]] = state[s["p"]] * s["p_last"] + jnp.where(lo, ds[:l], ds[l:])
    for p in range(n_pairs):
        s_ref[nxt, p] = state[p]

    for s in st:
        rows, cols = s["rows"], s["cols"]
        y = s["y"]
        inv_hd = 1.0 / hd
        mu = half_sums(y) * inv_hd
        yc = y - mu
        var = half_sums(yc * yc) * inv_hd
        yn = yc * lax.rsqrt(var + RW_LNX_EPS) * lnw_ref[:, cols] + lnb_ref[:, cols]
        bonus = half_sums(r_ref[rows, cols].astype(F32) * s["kmod"] * rk_ref[:, cols])
        yn = yn + bonus * s["v"]
        out_ref[rows, cols] = (yn * g_ref[rows, cols].astype(F32)).astype(out_ref.dtype)


def _rwkv_scan(r, k, v, lw, a, g, k_k, k_a, r_k, lnx_w, lnx_b, bsz, seq, l, n_sub):
    t, d = r.shape
    assert seq % (l * n_sub) == 0
    nc = seq // (l * n_sub)
    blk = pl.BlockSpec((l * n_sub, d), lambda b, c: (b * nc + c, 0))
    vec = pl.BlockSpec((1, d), lambda b, c: (0, 0))
    vecs = [a_.reshape(1, d) for a_ in (k_k, k_a, r_k, lnx_w, lnx_b)]
    return pl.pallas_call(
        functools.partial(_rwkv_scan_kernel, l=l),
        grid=(bsz, nc),
        in_specs=[blk] * 6 + [vec] * 5,
        out_specs=blk,
        out_shape=jax.ShapeDtypeStruct((t, d), BF16),
        scratch_shapes=[pltpu.VMEM((2, d // LANES, RW_HEADDIM, LANES), F32)],
        compiler_params=_params(("parallel", "arbitrary")),
        name="rwkv_scan",
    )(r, k, v, lw, a, g, *vecs)


def _rwkv_layer(x, res, mix, w_rkv, w0, w1, w2, a0, a1, a2, g1, g2, k_k, k_a, r_k,
                lnx_w, lnx_b, w_out, ln_g, ln_b, bsz, seq):
    r, k, v, lw, a, g = _rwkv_prep(x, mix, w_rkv, w0, w1, w2, a0, a1, a2, g1, g2, seq, 512)
    y = _rwkv_scan(r, k, v, lw, a, g, k_k, k_a, r_k, lnx_w, lnx_b, bsz, seq, RW_CHUNK, 2)
    return _out_ln(y, w_out.astype(BF16), res, ln_g, ln_b, 1024, 256, "rwkv_out_ln")


def kernel(x, ln_g, ln_b, ffn_w_in, ffn_w_out, ml_w_in, ml_b_gate, ml_norm_w, ml_w_out, ssm_w_in, ssm_conv_w, ssm_conv_b, ssm_dt_bias, ssm_a_log, ssm_d, ssm_norm_w, ssm_w_out, rw_mix, rw_w_rkv, rw_w0, rw_w1, rw_w2, rw_a0, rw_a1, rw_a2, rw_g1, rw_g2, rw_k_k, rw_k_a, rw_r_k, rw_lnx_w, rw_lnx_b, rw_w_out):
    bsz, seq, d = x.shape
    h = x.reshape(bsz * seq, d)
    for i in range(DEPTH):
        kind, j = i % N_MIXERS, i // N_MIXERS
        if kind == 0:
            h = _mlstm_layer(h, h, ml_w_in[j], ml_b_gate[j], ml_norm_w[j], ml_w_out[j],
                             ln_g[i, 0], ln_b[i, 0], bsz, seq)
        elif kind == 1:
            h = _mamba_layer(h, h, ssm_w_in[j], ssm_conv_w[j], ssm_conv_b[j], ssm_dt_bias[j],
                             ssm_a_log[j], ssm_d[j], ssm_norm_w[j], ssm_w_out[j],
                             ln_g[i, 0], ln_b[i, 0], bsz, seq)
        else:
            h = _rwkv_layer(h, h, rw_mix[j], rw_w_rkv[j], rw_w0[j], rw_w1[j], rw_w2[j],
                            rw_a0[j], rw_a1[j], rw_a2[j], rw_g1[j], rw_g2[j], rw_k_k[j],
                            rw_k_a[j], rw_r_k[j].reshape(-1), rw_lnx_w[j], rw_lnx_b[j],
                            rw_w_out[j], ln_g[i, 0], ln_b[i, 0], bsz, seq)
        h = _ffn_ln(h, ffn_w_in[i].astype(BF16), ffn_w_out[i].astype(BF16),
                    ln_g[i, 1], ln_b[i, 1], 1024, 256, 512)
    return h.reshape(bsz, seq, d)
```

```python
import functools
import math

import jax
import jax.numpy as jnp
from jax import lax
from jax.experimental import pallas as pl
from jax.experimental.pallas import tpu as pltpu

F32 = jnp.float32
BF16 = jnp.bfloat16

D_MODEL = 1024
DEPTH = 4
N_MIXERS = 3
DEEPNORM_ALPHA = (2 * DEPTH) ** 0.25
LOG2E = 1.0 / math.log(2.0)
EXP2_CAP = 126.0
LN_EPS = 1e-5
RMS_EPS = 1e-6

ML_HEADS = 4
ML_DV = D_MODEL // ML_HEADS
ML_DQK = ML_DV // 2
ML_GATE_CAP = 15.0
ML_QKVO = 2 * ML_HEADS * ML_DQK + 2 * ML_HEADS * ML_DV

SSM_DINNER = 2 * D_MODEL
SSM_HEADDIM = 64
SSM_HEADS = SSM_DINNER // SSM_HEADDIM
SSM_STATE = 128
SSM_GROUPS = 4
SSM_CONV = 4
SSM_HPG = SSM_HEADS // SSM_GROUPS
SSM_GW = SSM_DINNER // SSM_GROUPS
SSM_ZX = 2 * SSM_DINNER + 2 * SSM_GROUPS * SSM_STATE

RW_HEADDIM = 64
RW_HEADS = D_MODEL // RW_HEADDIM
RW_LNX_EPS = 64e-5

FFN_HIDDEN = int(math.ceil(8 * D_MODEL / 3 / 256)) * 256

LANES = 128
SUBLANES = 8
VMEM_LIMIT = 56 * 1024 * 1024

ML_CHUNK = 256
SSM_CHUNK = 128
SSM_TAIL = 2 * SUBLANES
RW_CHUNK = 64


def _params(sem):
    return pltpu.CompilerParams(dimension_semantics=sem, vmem_limit_bytes=VMEM_LIMIT)


def _dot(a, b, dims=((1,), (0,))):
    return lax.dot_general(a.astype(BF16), b.astype(BF16), (dims, ((), ())),
                           preferred_element_type=F32)


def _split3(a):
    hi = a.astype(BF16)
    r1 = a - hi.astype(F32)
    mid = r1.astype(BF16)
    lo = (r1 - mid.astype(F32)).astype(BF16)
    return hi, mid, lo


def _dot_sel(a, sel, dims=((1,), (0,))):
    sel = sel.astype(BF16)
    return sum(lax.dot_general(p, sel, (dims, ((), ())), preferred_element_type=F32)
               for p in _split3(a))


def _sel_dot(sel, a, dims=((1,), (0,))):
    sel = sel.astype(BF16)
    return sum(lax.dot_general(sel, p, (dims, ((), ())), preferred_element_type=F32)
               for p in _split3(a))


def _softplus(x):
    return jnp.maximum(x, 0.0) + jnp.log1p(jnp.exp(-jnp.abs(x)))


def _sigmoid(x):
    return 0.5 * jnp.tanh(0.5 * x) + 0.5


def _silu(x):
    return x * _sigmoid(x)


def _layer_norm(x, g, b):
    mu = jnp.mean(x, axis=-1, keepdims=True)
    xc = x - mu
    var = jnp.mean(xc * xc, axis=-1, keepdims=True)
    return xc * lax.rsqrt(var + LN_EPS) * g + b


def _tri(n, strict=False, upper=False):
    r = lax.broadcasted_iota(jnp.int32, (n, n), 0)
    c = lax.broadcasted_iota(jnp.int32, (n, n), 1)
    if upper:
        r, c = c, r
    return (c < r) if strict else (c <= r)


def _resident(a):
    return pl.BlockSpec(a.shape, lambda *_: (0,) * a.ndim, pipeline_mode=pl.Buffered(1))


def _in_proj_kernel(x_ref, w_ref, ws_ref, o_ref, os_ref, *, tn):
    x = x_ref[...]
    xb = x.astype(BF16)
    cols = [slice(j * tn, (j + 1) * tn) for j in range(w_ref.shape[1] // tn)]
    acc = _dot(xb, w_ref[:, cols[0]])
    for j in range(len(cols)):
        nxt = _dot(xb, w_ref[:, cols[j + 1]]) if j + 1 < len(cols) else None
        o_ref[:, cols[j]] = acc.astype(o_ref.dtype)
        acc = nxt
    x_lo = (x - xb.astype(F32)).astype(BF16)
    both = _dot(xb, ws_ref[...])
    os_ref[...] = both[:, :LANES] + both[:, LANES:] + _dot(x_lo, ws_ref[:, :LANES])


def _in_proj(x, w, n_main, tm, tn, name):
    t, k = x.shape
    assert t % tm == 0 and n_main % tn == 0 and w.shape[1] - n_main <= LANES
    w_main = w[:, :n_main].astype(BF16)
    w_few = jnp.pad(w[:, n_main:], ((0, 0), (0, LANES - (w.shape[1] - n_main))))
    w_hi = w_few.astype(BF16)
    w_lo = (w_few - w_hi.astype(F32)).astype(BF16)
    w_split = jnp.concatenate([w_hi, w_lo], axis=1)
    return pl.pallas_call(
        functools.partial(_in_proj_kernel, tn=tn),
        grid=(t // tm,),
        in_specs=[pl.BlockSpec((tm, k), lambda i: (i, 0)), _resident(w_main), _resident(w_split)],
        out_specs=[pl.BlockSpec((tm, n_main), lambda i: (i, 0)),
                   pl.BlockSpec((tm, LANES), lambda i: (i, 0))],
        out_shape=[jax.ShapeDtypeStruct((t, n_main), BF16),
                   jax.ShapeDtypeStruct((t, LANES), F32)],
        compiler_params=_params(("parallel",)),
        name=name,
    )(x, w_main, w_split)


def _out_ln_kernel(y_ref, w_ref, res_ref, g_ref, b_ref, o_ref, *, ts):
    n_sub = y_ref.shape[0] // ts
    rows = [slice(s * ts, (s + 1) * ts) for s in range(n_sub)]
    acc = _dot(y_ref[rows[0], :], w_ref[...])
    for s in range(n_sub):
        nxt = _dot(y_ref[rows[s + 1], :], w_ref[...]) if s + 1 < n_sub else None
        o_ref[rows[s], :] = _layer_norm(DEEPNORM_ALPHA * res_ref[rows[s], :] + acc,
                                        g_ref[...], b_ref[...])
        acc = nxt


def _out_ln(y, w, res, g, b, tm, ts, name):
    t, k = y.shape
    d = w.shape[1]
    assert tm % ts == 0
    g2, b2 = g.reshape(1, d), b.reshape(1, d)
    return pl.pallas_call(
        functools.partial(_out_ln_kernel, ts=ts),
        grid=(t // tm,),
        in_specs=[pl.BlockSpec((tm, k), lambda i: (i, 0)),
                  _resident(w),
                  pl.BlockSpec((tm, d), lambda i: (i, 0)),
                  _resident(g2), _resident(b2)],
        out_specs=pl.BlockSpec((tm, d), lambda i: (i, 0)),
        out_shape=jax.ShapeDtypeStruct((t, d), F32),
        compiler_params=_params(("parallel",)),
        name=name,
    )(y, w, res, g2, b2)


def _ffn_kernel(x_ref, wi_ref, wo_ref, g_ref, b_ref, o_ref, *, tf, ts):
    f = wo_ref.shape[0]
    nf = f // tf

    def finish(rows, x, acc):
        o_ref[rows, :] = _layer_norm(DEEPNORM_ALPHA * x + acc, g_ref[...], b_ref[...])

    pending = None
    for s in range(x_ref.shape[0] // ts):
        rows = slice(s * ts, (s + 1) * ts)
        x = x_ref[rows, :]
        xb = x.astype(BF16)
        gate_up = lambda j: (_dot(xb, wi_ref[:, j * tf:(j + 1) * tf]),
                             _dot(xb, wi_ref[:, f + j * tf:f + (j + 1) * tf]))
        acc = None
        nxt = gate_up(0)
        if pending is not None:
            finish(*pending)
        for j in range(nf):
            gate, up = nxt
            if j + 1 < nf:
                nxt = gate_up(j + 1)
            part = _dot(_silu(gate) * up, wo_ref[j * tf:(j + 1) * tf, :])
            acc = part if acc is None else acc + part
        pending = (rows, x, acc)
    finish(*pending)


def _ffn_ln(x, w_in, w_out, g, b, tm, tf, ts):
    t, d = x.shape
    f = w_out.shape[0]
    assert f % tf == 0 and tf % LANES == 0 and tm % ts == 0
    g2, b2 = g.reshape(1, d), b.reshape(1, d)
    return pl.pallas_call(
        functools.partial(_ffn_kernel, tf=tf, ts=ts),
        grid=(t // tm,),
        in_specs=[pl.BlockSpec((tm, d), lambda i: (i, 0)),
                  _resident(w_in), _resident(w_out), _resident(g2), _resident(b2)],
        out_specs=pl.BlockSpec((tm, d), lambda i: (i, 0)),
        out_shape=jax.ShapeDtypeStruct((t, d), F32),
        compiler_params=_params(("parallel",)),
        name="ffn_ln",
    )(x, w_in, w_out, g2, b2)


def _rows_to_chunk_rows(a, bsz, nc, l):
    c = a.shape[1]
    return a.reshape(bsz, nc, l, c).transpose(0, 1, 3, 2)


def _mlstm_kernel(q_ref, k_ref, v_ref, o_ref, gc_ref, gr_ref, bc_ref, br_ref, nw_ref,
                  out_ref, c_ref, n_ref, m_ref, *, l):
    h_, dk, dv = ML_HEADS, ML_DQK, ML_DV

    cur = pl.program_id(1) % 2
    nxt = 1 - cur

    @pl.when(pl.program_id(1) == 0)
    def _():
        c_ref[0] = jnp.zeros(c_ref.shape[1:], F32)
        n_ref[0] = jnp.zeros(n_ref.shape[1:], F32)
        m_ref[0] = jnp.zeros(m_ref.shape[1:], F32)

    cap = lambda t: ML_GATE_CAP * jnp.tanh(t / ML_GATE_CAP)
    gc = cap(gc_ref[...] + bc_ref[...])
    gr = cap(gr_ref[...] + br_ref[...])
    i_col, lf_col = gc[:, :h_], -_softplus(-gc[:, h_:])
    i_row, lf_row = gr[:h_, :], -_softplus(-gr[h_:, :])
    causal = _tri(l)
    b_col = _sel_dot(causal, lf_col)
    b_row = _dot_sel(lf_row, _tri(l, upper=True))
    scale = dk ** -0.5

    heads = range(h_)
    qs = [q_ref[:, h * dk:(h + 1) * dk] for h in heads]
    ks = [k_ref[:, h * dk:(h + 1) * dk] for h in heads]
    vs = [v_ref[:, h * dv:(h + 1) * dv] for h in heads]
    s_qk = [_dot(qs[h], ks[h], ((1,), (1,))) for h in heads]
    c_prev = [c_ref[cur, h] for h in heads]
    q_c = [_dot(qs[h], c_prev[h]) for h in heads]
    m_prev = [m_ref[cur, h][:, :1] for h in heads]
    og = [nw_ref[:, h * dv:(h + 1) * dv] * _sigmoid(o_ref[:, h * dv:(h + 1) * dv].astype(F32))
          for h in heads]

    m_t, wts, sc = [], [], []
    for h in heads:
        bc, br, ir = b_col[:, h:h + 1], b_row[h:h + 1, :], i_row[h:h + 1, :]
        dmat = jnp.where(causal, bc - br + ir, -jnp.inf)
        inter = bc + m_prev[h]
        m_t.append(jnp.maximum(inter, jnp.max(dmat, axis=1, keepdims=True)))
        wts.append(jnp.exp(dmat - m_t[h]) * scale)
        sc.append(jnp.exp(inter - m_t[h]) * scale)

    qk = [s_qk[h] * wts[h] for h in heads]
    num = [_dot(qk[h], vs[h]) + sc[h] * q_c[h] for h in heads]

    kw, dec, m_new = [], [], []
    for h in heads:
        bc, br = b_col[:, h:h + 1], b_row[h:h + 1, :]
        b_last = bc[l - 1:l, :]
        g_row = b_last - br + i_row[h:h + 1, :]
        g_col = b_last - bc + i_col[:, h:h + 1]
        m_new.append(jnp.maximum(b_last + m_prev[h], jnp.max(g_row, axis=1, keepdims=True)))
        kw.append(ks[h].astype(F32) * jnp.exp(g_col - m_new[h]))
        dec.append(jnp.exp(b_last + m_prev[h] - m_new[h]))
    kv = [_dot(kw[h], vs[h], ((0,), (0,))) for h in heads]

    n_prev = [n_ref[cur, h] for h in heads]
    for h in heads:
        c_ref[nxt, h] = dec[h] * c_prev[h] + kv[h]
        n_ref[nxt, h] = dec[h] * n_prev[h] + jnp.sum(kw[h], axis=0, keepdims=True)
        m_ref[nxt, h] = jnp.broadcast_to(m_new[h], (1, LANES))
    qk_sum = [jnp.sum(qk[h], axis=1, keepdims=True) for h in heads]
    qn_sum = [jnp.sum(qs[h].astype(F32) * n_prev[h], axis=1, keepdims=True) for h in heads]
    hh = [num[h] / jnp.maximum(jnp.abs(qk_sum[h] + sc[h] * qn_sum[h]), jnp.exp(-m_t[h]))
          for h in heads]
    ms = [jnp.mean(hh[h] * hh[h], axis=1, keepdims=True) for h in heads]
    for h in heads:
        sl = slice(h * dv, (h + 1) * dv)
        hn = hh[h] * lax.rsqrt(ms[h] + RMS_EPS)
        out_ref[:, sl] = (hn * og[h]).astype(out_ref.dtype)


def _mlstm_core(proj, gpre, b_gate, norm_w, bsz, seq, l):
    h_, dk, dv = ML_HEADS, ML_DQK, ML_DV
    nc = seq // l
    t = bsz * seq
    row = lambda b, c: (b * nc + c, 0)
    g_rows = _rows_to_chunk_rows(gpre, bsz, nc, l)
    return pl.pallas_call(
        functools.partial(_mlstm_kernel, l=l),
        grid=(bsz, nc),
        in_specs=[pl.BlockSpec((l, h_ * dk), lambda b, c: (b * nc + c, 0)),
                  pl.BlockSpec((l, h_ * dk), lambda b, c: (b * nc + c, 1)),
                  pl.BlockSpec((l, h_ * dv), lambda b, c: (b * nc + c, 1)),
                  pl.BlockSpec((l, h_ * dv), lambda b, c: (b * nc + c, 2)),
                  pl.BlockSpec((l, 2 * h_), row),
                  pl.BlockSpec((None, None, 2 * h_, l), lambda b, c: (b, c, 0, 0)),
                  pl.BlockSpec((1, 2 * h_), lambda b, c: (0, 0)),
                  pl.BlockSpec((2 * h_, 1), lambda b, c: (0, 0)),
                  pl.BlockSpec((1, h_ * dv), lambda b, c: (0, 0))],
        out_specs=pl.BlockSpec((l, h_ * dv), row),
        out_shape=jax.ShapeDtypeStruct((t, h_ * dv), BF16),
        scratch_shapes=[pltpu.VMEM((2, h_, dk, dv), F32),
                        pltpu.VMEM((2, h_, 1, dk), F32),
                        pltpu.VMEM((2, h_, 1, LANES), F32)],
        compiler_params=_params(("parallel", "arbitrary")),
        name="mlstm_core",
    )(proj, proj, proj, proj, gpre, g_rows, b_gate.reshape(1, 2 * h_),
      b_gate.reshape(2 * h_, 1), norm_w.reshape(1, h_ * dv))


def _mlstm_layer(x, res, w_in, b_gate, norm_w, w_out, ln_g, ln_b, bsz, seq):
    proj, gpre = _in_proj(x, w_in, ML_QKVO, 512, 1024, "mlstm_in")
    hn = _mlstm_core(proj, gpre[:, :2 * ML_HEADS], b_gate, norm_w, bsz, seq, ML_CHUNK)
    return _out_ln(hn, w_out.astype(BF16), res, ln_g, ln_b, 1024, 256, "mlstm_out_ln")


def _ssd_kernel(z_ref, x_ref, bm_ref, cm_ref, dtc_ref, dtr_ref,
                cwx_ref, cwb_ref, cwc_ref, cbx_ref, cbb_ref, cbc_ref,
                dbc_ref, dbr_ref, alc_ref, alr_ref, dsk_ref, nw_ref,
                out_ref, st_ref, px_ref, pb_ref, pc_ref, *, l):
    g_, hpg, p, n, gw = SSM_GROUPS, SSM_HPG, SSM_HEADDIM, SSM_STATE, SSM_GW
    assert l == LANES and 2 * p == LANES
    tail = SSM_TAIL

    cur = pl.program_id(1) % 2
    nxt = 1 - cur

    @pl.when(pl.program_id(1) == 0)
    def _():
        st_ref[0] = jnp.zeros(st_ref.shape[1:], F32)
        px_ref[0] = jnp.zeros(px_ref.shape[1:], BF16)
        pb_ref[0] = jnp.zeros(pb_ref.shape[1:], BF16)
        pc_ref[0] = jnp.zeros(pc_ref.shape[1:], BF16)

    taps = SSM_CONV
    sel_r = lax.broadcasted_iota(jnp.int32, (taps * l, l + tail), 0)
    sel_c = lax.broadcasted_iota(jnp.int32, (taps * l, l + tail), 1)
    want_c = sel_r + (tail - (taps - 1))
    for j in range(1, taps):
        want_c = jnp.where(sel_r >= j * l, sel_r + (tail - (taps - 1) + j - j * l), want_c)
    shift_sel = jnp.where(sel_c == want_c, 1.0, 0.0).astype(BF16)

    def conv_silu(tail_ref, cur_ref, w_ref, b_ref):
        now = cur_ref[...]
        sh = lax.dot_general(shift_sel, jnp.concatenate([tail_ref[cur], now], axis=0),
                             (((1,), (0,)), ((), ())), preferred_element_type=F32)
        tail_ref[nxt] = now[l - tail:, :]
        acc = b_ref[...]
        for j in range(taps):
            acc = acc + w_ref[j:j + 1, :] * sh[j * l:(j + 1) * l]
        return _silu(acc)

    dt_c = _softplus(dtc_ref[...] + dbc_ref[...])
    dt_r = _softplus(dtr_ref[...] + dbr_ref[...])
    causal = _tri(l)
    cum_c = _sel_dot(causal, dt_c * -jnp.exp(alc_ref[...])) * LOG2E
    cum_r = _dot_sel(dt_r * -jnp.exp(alr_ref[...]), _tri(l, upper=True)) * LOG2E
    sub_r = cum_r - jnp.log(dt_r) * LOG2E
    lo = lax.broadcasted_iota(jnp.int32, (l, LANES), 1) < p
    splat = lambda a, c: jnp.broadcast_to(a[:, c:c + 1], (a.shape[0], LANES))

    b_all = conv_silu(pb_ref, bm_ref, cwb_ref, cbb_ref).astype(BF16)
    c_all = conv_silu(pc_ref, cm_ref, cwc_ref, cbc_ref).astype(BF16)
    x_all = conv_silu(px_ref, x_ref, cwx_ref, cbx_ref)

    for g in range(g_):
        ncols = slice(g * n, (g + 1) * n)
        bg, cg = b_all[:, ncols], c_all[:, ncols]
        cbm = jnp.where(causal, _dot(cg, bg, ((1,), (1,))), 0.0)
        st = st_ref[cur, g]
        y_inter = _dot(cg, st)

        xs, ys, xw, dec = [], [], [], []
        for pr in range(hpg // 2):
            he, ho = g * hpg + 2 * pr, g * hpg + 2 * pr + 1
            cols = slice(g * gw + pr * LANES, g * gw + (pr + 1) * LANES)
            x_pair = x_all[:, cols]
            cc_e, cc_o = splat(cum_c, he), splat(cum_c, ho)
            mm_e = cbm * jnp.exp2(jnp.minimum(cc_e - sub_r[he:he + 1, :], EXP2_CAP))
            mm_o = cbm * jnp.exp2(jnp.minimum(cc_o - sub_r[ho:ho + 1, :], EXP2_CAP))
            yy = _dot(jnp.concatenate([mm_e, mm_o], axis=0), x_pair)
            cc = jnp.where(lo, cc_e, cc_o)
            c_last = cc[l - 1:l, :]
            w_s = jnp.exp2(c_last - cc) * jnp.where(lo, splat(dt_c, he), splat(dt_c, ho))
            xs.append(x_pair)
            xw.append((x_pair * w_s).astype(BF16))
            dec.append(jnp.exp2(c_last))
            ys.append(jnp.where(lo, yy[:l], yy[l:])
                      + y_inter[:, pr * LANES:(pr + 1) * LANES] * jnp.exp2(cc))

        gcols = slice(g * gw, (g + 1) * gw)
        st_ref[nxt, g] = (st * jnp.concatenate(dec, axis=1)
                     + _dot(bg, jnp.concatenate(xw, axis=1), ((0,), (0,))))
        y = jnp.concatenate(ys, axis=1) + jnp.concatenate(xs, axis=1) * dsk_ref[:, gcols]
        y = y * _silu(z_ref[:, gcols].astype(F32))
        y = y * lax.rsqrt(jnp.mean(y * y, axis=1, keepdims=True) + RMS_EPS)
        out_ref[:, gcols] = (y * nw_ref[:, gcols]).astype(out_ref.dtype)


def _ssd_core(proj, dtpre, conv_w, conv_b, dt_bias, a_log, d_skip, norm_w, bsz, seq, l):
    g_, p, n, h_ = SSM_GROUPS, SSM_HEADDIM, SSM_STATE, SSM_HEADS
    di, gn = SSM_DINNER, SSM_GROUPS * SSM_STATE
    nc = seq // l
    t = bsz * seq
    dt_rows = _rows_to_chunk_rows(dtpre, bsz, nc, l)
    d_cols = jnp.repeat(d_skip, p).reshape(1, di)
    cwx, cwb, cwc = conv_w[:, :di], conv_w[:, di:di + gn], conv_w[:, di + gn:]
    cb2 = conv_b.reshape(1, -1)
    cbx, cbb, cbc = cb2[:, :di], cb2[:, di:di + gn], cb2[:, di + gn:]
    row = lambda b, c: b * nc + c
    full = lambda a: pl.BlockSpec(a.shape, lambda b, c: (0,) * a.ndim)
    consts = [cwx, cwb, cwc, cbx, cbb, cbc, dt_bias.reshape(1, h_), dt_bias.reshape(h_, 1),
              a_log.reshape(1, h_), a_log.reshape(h_, 1), d_cols, norm_w.reshape(1, di)]
    return pl.pallas_call(
        functools.partial(_ssd_kernel, l=l),
        grid=(bsz, nc),
        in_specs=[pl.BlockSpec((l, di), lambda b, c: (row(b, c), 0)),
                  pl.BlockSpec((l, di), lambda b, c: (row(b, c), 1)),
                  pl.BlockSpec((l, gn), lambda b, c: (row(b, c), 2 * di // gn)),
                  pl.BlockSpec((l, gn), lambda b, c: (row(b, c), 2 * di // gn + 1)),
                  pl.BlockSpec((l, h_), lambda b, c: (row(b, c), 0)),
                  pl.BlockSpec((None, None, h_, l), lambda b, c: (b, c, 0, 0))]
                 + [full(a) for a in consts],
        out_specs=pl.BlockSpec((l, di), lambda b, c: (row(b, c), 0)),
        out_shape=jax.ShapeDtypeStruct((t, di), BF16),
        scratch_shapes=[pltpu.VMEM((2, g_, n, SSM_GW), F32),
                        pltpu.VMEM((2, SSM_TAIL, di), BF16),
                        pltpu.VMEM((2, SSM_TAIL, gn), BF16),
                        pltpu.VMEM((2, SSM_TAIL, gn), BF16)],
        compiler_params=_params(("parallel", "arbitrary")),
        name="ssd_core",
    )(proj, proj, proj, proj, dtpre, dt_rows, *consts)


def _mamba_layer(x, res, w_in, conv_w, conv_b, dt_bias, a_log, d_skip, norm_w, w_out,
                 ln_g, ln_b, bsz, seq):
    proj, dtpre = _in_proj(x, w_in, SSM_ZX, 512, 1024, "ssm_in")
    y = _ssd_core(proj, dtpre[:, :SSM_HEADS], conv_w, conv_b, dt_bias, a_log, d_skip,
                  norm_w, bsz, seq, SSM_CHUNK)
    return _out_ln(y, w_out.astype(BF16), res, ln_g, ln_b, 1024, 256, "ssm_out_ln")


def _rwkv_prep_kernel(x_ref, xl_ref, mix_ref, wrkv_ref, w0_ref, w1_ref, w2_ref,
                      a0_ref, a1_ref, a2_ref, g1_ref, g2_ref,
                      r_ref, k_ref, v_ref, lw_ref, a_ref, g_ref, *, tiles_per_seq):
    x = x_ref[...]
    first = pl.program_id(0) % tiles_per_seq == 0
    before = jnp.where(first, 0.0, xl_ref[SUBLANES - 1:SUBLANES, :])
    row = lax.broadcasted_iota(jnp.int32, x.shape, 0)
    xx = jnp.where(row == 0, before, pltpu.roll(x, 1, axis=0)) - x
    mixed = lambda j: x + xx * mix_ref[j:j + 1, :]
    r_ref[...] = _dot(mixed(0), wrkv_ref[0]).astype(r_ref.dtype)
    k_ref[...] = _dot(mixed(2), wrkv_ref[1]).astype(k_ref.dtype)
    v_ref[...] = _dot(mixed(3), wrkv_ref[2]).astype(v_ref.dtype)
    w_raw = w0_ref[...] + _dot(jnp.tanh(_dot(mixed(1), w1_ref[...])), w2_ref[...])
    w = -_softplus(-w_raw) - 0.5
    lw_ref[...] = -jnp.exp(w)
    a_ref[...] = _sigmoid(a0_ref[...] + _dot(_dot(mixed(4), a1_ref[...]), a2_ref[...])
                          ).astype(a_ref.dtype)
    g_ref[...] = _dot(_sigmoid(_dot(mixed(5), g1_ref[...])), g2_ref[...]).astype(g_ref.dtype)


def _rwkv_prep(x, mix, w_rkv, w0, w1, w2, a0, a1, a2, g1, g2, seq, tm):
    t, d = x.shape
    assert seq % tm == 0 and tm % SUBLANES == 0
    rowblk = pl.BlockSpec((tm, d), lambda i: (i, 0))
    prev8 = pl.BlockSpec((SUBLANES, d), lambda i: (jnp.maximum(i * (tm // SUBLANES) - 1, 0), 0))
    ws = [mix, w_rkv.astype(BF16), w0.reshape(1, d), w1.astype(BF16), w2.astype(BF16),
          a0.reshape(1, d), a1.astype(BF16), a2.astype(BF16), g1.astype(BF16),
          g2.astype(BF16)]
    dts = [BF16, BF16, BF16, F32, BF16, BF16]
    return pl.pallas_call(
        functools.partial(_rwkv_prep_kernel, tiles_per_seq=seq // tm),
        grid=(t // tm,),
        in_specs=[rowblk, prev8] + [_resident(a) for a in ws],
        out_specs=[rowblk] * 6,
        out_shape=[jax.ShapeDtypeStruct((t, d), dt) for dt in dts],
        compiler_params=_params(("parallel",)),
        name="rwkv_prep",
    )(x, x, *ws)


def _rwkv_scan_kernel(r_ref, k_ref, v_ref, lw_ref, a_ref, g_ref,
                      kk_ref, ka_ref, rk_ref, lnw_ref, lnb_ref,
                      out_ref, s_ref, *, l):
    hd = RW_HEADDIM
    assert l == hd and 2 * hd == LANES
    n_pairs = r_ref.shape[1] // LANES

    cur = pl.program_id(1) % 2
    nxt = 1 - cur

    @pl.when(pl.program_id(1) == 0)
    def _():
        s_ref[0] = jnp.zeros(s_ref.shape[1:], F32)

    lane =lax.broadcasted_iota(jnp.int32, (l, LANES), 1)
    tok_t = lax.broadcasted_iota(jnp.int32, (l, LANES), 0)
    tok_s = lane & (hd - 1)
    lo = lane < hd
    strict2 = tok_s < tok_t
    incl2 = tok_s <= tok_t
    mlo = lambda x: jnp.where(lo, x, 0.0)
    mhi = lambda x: jnp.where(lo, 0.0, x)
    swap = lambda x: pltpu.roll(x, hd, axis=1)
    cat0 = lambda *xs: jnp.concatenate(xs, axis=0)
    nt = ((1,), (1,))
    tn = ((0,), (0,))

    def half_sums(x):
        s_lo = jnp.sum(mlo(x), axis=1, keepdims=True)
        s_hi = jnp.sum(mhi(x), axis=1, keepdims=True)
        return jnp.where(lo, s_lo, s_hi)

    steps = l.bit_length() - 1
    n_sub = r_ref.shape[0] // l

    st = [dict(p=p, rows=slice(c * l, (c + 1) * l), cols=slice(p * LANES, (p + 1) * LANES))
          for c in range(n_sub) for p in range(n_pairs)]
    for s in st:
        s["lw"] = lw_ref[s["rows"], s["cols"]]
    for s in st:
        gcum = s["lw"]
        for k in range(steps):
            gcum = gcum + jnp.where(tok_t >= 2 ** k, pltpu.roll(gcum, 2 ** k, axis=0), 0.0)
        s["gcum"] = gcum
    for s in st:
        rows, cols = s["rows"], s["cols"]
        gcum, lw = s["gcum"], s.pop("lw")
        a_sig = a_ref[rows, cols].astype(F32)
        k_in = k_ref[rows, cols].astype(F32)
        g_last = gcum[l - 1:l, :]
        p_inv = jnp.exp(-gcum)
        p_tail = jnp.exp(g_last - gcum)
        kmod = k_in * (1.0 + (a_sig - 1.0) * ka_ref[:, cols])
        kk = k_in * kk_ref[:, cols]
        kk = kk * lax.rsqrt(jnp.maximum(half_sums(kk * kk), 1e-24))
        b_vec = kk * a_sig
        a_dec = -kk * jnp.exp(gcum - lw)
        r_dec = r_ref[rows, cols].astype(F32) * jnp.exp(gcum)
        b_inv, k_inv = b_vec * p_inv, kmod * p_inv
        s.update(a_dec=a_dec, r_dec=r_dec, kmod=kmod, p_last=jnp.exp(g_last),
                 bk_tail=cat0(b_vec * p_tail, kmod * p_tail).astype(BF16),
                 bk_inv=cat0(mlo(b_inv), mlo(k_inv), mhi(b_inv), mhi(k_inv)).astype(BF16))
        del s["gcum"]
    for s in st:
        s["pm"] = _dot(cat0(s["a_dec"], s["r_dec"]), s.pop("bk_inv"), nt)
    for s in st:
        pm = s.pop("pm")
        s["a_e"] = jnp.where(strict2, pm[:l, :LANES], 0.0)
        s["a_o"] = jnp.where(strict2, pm[:l, LANES:], 0.0)
        s["m_eo"] = cat0(jnp.where(incl2, pm[l:, :LANES], 0.0),
                         jnp.where(incl2, pm[l:, LANES:], 0.0)).astype(BF16)
        s["v"] = v_ref[s["rows"], s["cols"]].astype(F32)
        s["v_sw"] = swap(s["v"])
    for s in st:
        s["akv"] = _dot(cat0(s["a_e"], s["a_o"]), cat0(jnp.zeros_like(s["v_sw"]), s["v_sw"]))
    for s in st:
        akv, a_dec = s.pop("akv"), s.pop("a_dec")
        s["xx"] = cat0(mlo(a_dec) + mhi(akv[:l]), mhi(a_dec) + mlo(akv[l:]))
        s["n_bd"] = cat0(mlo(s.pop("a_e")), swap(mlo(s.pop("a_o"))))

    for j in range(steps):
        for s in st:
            nb = s["n_bd"].astype(BF16)
            if j + 1 < steps:
                out = _dot(nb, jnp.concatenate([s["xx"].astype(BF16), nb], axis=1))
                s["xx"] = s["xx"] + out[:, :LANES]
                s["n_bd"] = out[:, LANES:]
            else:
                s["xx"] = s["xx"] + _dot(nb, s["xx"])

    state = [s_ref[cur, p] for p in range(n_pairs)]
    for c in range(n_sub):
        sub = st[c * n_pairs:(c + 1) * n_pairs]
        for s in sub:
            x_e, x_o = s["xx"][:l], s["xx"][l:]
            s_pair = state[s["p"]]
            s["tt"] = _dot(cat0(mlo(x_e) + mhi(x_o), s["r_dec"]),
                           cat0(mlo(s_pair), mhi(s_pair)), nt)
        for s in sub:
            tt = s.pop("tt")
            x_e, x_o = s["xx"][:l], s["xx"][l:]
            uu = tt[:l] + swap(mhi(x_e) + mlo(x_o))
            s["y0"] = tt[l:]
            s["uv"] = cat0(uu, s["v"]).astype(BF16)
        for s in sub:
            s["yc"] = _dot(s["m_eo"], s["uv"])
            s["ds"] = _dot(s["uv"], s["bk_tail"], tn)
        for s in sub:
            yc, ds = s.pop("yc"), s.pop("ds")
            s["y"] = s["y0"] + mlo(yc[:l]) + mhi(yc[l:])
            state[s["p"]] = state[s["p"]] * s["p_last"] + jnp.where(lo, ds[:l], ds[l:])
    for p in range(n_pairs):
        s_ref[nxt, p] = state[p]

    for s in st:
        rows, cols = s["rows"], s["cols"]
        y = s["y"]
        inv_hd = 1.0 / hd
        mu = half_sums(y) * inv_hd
        yc = y - mu
        var = half_sums(yc * yc) * inv_hd
        yn = yc * lax.rsqrt(var + RW_LNX_EPS) * lnw_ref[:, cols] + lnb_ref[:, cols]
        bonus = half_sums(r_ref[rows, cols].astype(F32) * s["kmod"] * rk_ref[:, cols])
        yn = yn + bonus * s["v"]
        out_ref[rows, cols] = (yn * g_ref[rows, cols].astype(F32)).astype(out_ref.dtype)


def _rwkv_scan(r, k, v, lw, a, g, k_k, k_a, r_k, lnx_w, lnx_b, bsz, seq, l, n_sub):
    t, d = r.shape
    assert seq % (l * n_sub) == 0
    nc = seq // (l * n_sub)
    blk = pl.BlockSpec((l * n_sub, d), lambda b, c: (b * nc + c, 0))
    vec = pl.BlockSpec((1, d), lambda b, c: (0, 0))
    vecs = [a_.reshape(1, d) for a_ in (k_k, k_a, r_k, lnx_w, lnx_b)]
    return pl.pallas_call(
        functools.partial(_rwkv_scan_kernel, l=l),
        grid=(bsz, nc),
        in_specs=[blk] * 6 + [vec] * 5,
        out_specs=blk,
        out_shape=jax.ShapeDtypeStruct((t, d), BF16),
        scratch_shapes=[pltpu.VMEM((2, d // LANES, RW_HEADDIM, LANES), F32)],
        compiler_params=_params(("parallel", "arbitrary")),
        name="rwkv_scan",
    )(r, k, v, lw, a, g, *vecs)


def _rwkv_layer(x, res, mix, w_rkv, w0, w1, w2, a0, a1, a2, g1, g2, k_k, k_a, r_k,
                lnx_w, lnx_b, w_out, ln_g, ln_b, bsz, seq):
    r, k, v, lw, a, g = _rwkv_prep(x, mix, w_rkv, w0, w1, w2, a0, a1, a2, g1, g2, seq, 512)
    y = _rwkv_scan(r, k, v, lw, a, g, k_k, k_a, r_k, lnx_w, lnx_b, bsz, seq, RW_CHUNK, 2)
    return _out_ln(y, w_out.astype(BF16), res, ln_g, ln_b, 1024, 256, "rwkv_out_ln")


def kernel(x, ln_g, ln_b, ffn_w_in, ffn_w_out, ml_w_in, ml_b_gate, ml_norm_w, ml_w_out, ssm_w_in, ssm_conv_w, ssm_conv_b, ssm_dt_bias, ssm_a_log, ssm_d, ssm_norm_w, ssm_w_out, rw_mix, rw_w_rkv, rw_w0, rw_w1, rw_w2, rw_a0, rw_a1, rw_a2, rw_g1, rw_g2, rw_k_k, rw_k_a, rw_r_k, rw_lnx_w, rw_lnx_b, rw_w_out):
    bsz, seq, d = x.shape
    h = x.reshape(bsz * seq, d)
    for i in range(DEPTH):
        kind, j = i % N_MIXERS, i // N_MIXERS
        if kind == 0:
            h = _mlstm_layer(h, h, ml_w_in[j], ml_b_gate[j], ml_norm_w[j], ml_w_out[j],
                             ln_g[i, 0], ln_b[i, 0], bsz, seq)
        elif kind == 1:
            h = _mamba_layer(h, h, ssm_w_in[j], ssm_conv_w[j], ssm_conv_b[j], ssm_dt_bias[j],
                             ssm_a_log[j], ssm_d[j], ssm_norm_w[j], ssm_w_out[j],
                             ln_g[i, 0], ln_b[i, 0], bsz, seq)
        else:
            h = _rwkv_layer(h, h, rw_mix[j], rw_w_rkv[j], rw_w0[j], rw_w1[j], rw_w2[j],
                            rw_a0[j], rw_a1[j], rw_a2[j], rw_g1[j], rw_g2[j], rw_k_k[j],
                            rw_k_a[j], rw_r_k[j].reshape(-1), rw_lnx_w[j], rw_lnx_b[j],
                            rw_w_out[j], ln_g[i, 0], ln_b[i, 0], bsz, seq)
        h = _ffn_ln(h, ffn_w_in[i].astype(BF16), ffn_w_out[i].astype(BF16),
                    ln_g[i, 1], ln_b[i, 1], 1024, 256, 512)
    return h.reshape(bsz, seq, d)
```

```python
import functools
import math

import jax
import jax.numpy as jnp
from jax import lax
from jax.experimental import pallas as pl
from jax.experimental.pallas import tpu as pltpu

F32 = jnp.float32
BF16 = jnp.bfloat16

D_MODEL = 1024
DEPTH = 4
N_MIXERS = 3
DEEPNORM_ALPHA = (2 * DEPTH) ** 0.25
LOG2E = 1.0 / math.log(2.0)
EXP2_CAP = 126.0
LN_EPS = 1e-5
RMS_EPS = 1e-6

ML_HEADS = 4
ML_DV = D_MODEL // ML_HEADS
ML_DQK = ML_DV // 2
ML_GATE_CAP = 15.0
ML_QKVO = 2 * ML_HEADS * ML_DQK + 2 * ML_HEADS * ML_DV

SSM_DINNER = 2 * D_MODEL
SSM_HEADDIM = 64
SSM_HEADS = SSM_DINNER // SSM_HEADDIM
SSM_STATE = 128
SSM_GROUPS = 4
SSM_CONV = 4
SSM_HPG = SSM_HEADS // SSM_GROUPS
SSM_GW = SSM_DINNER // SSM_GROUPS
SSM_ZX = 2 * SSM_DINNER + 2 * SSM_GROUPS * SSM_STATE

RW_HEADDIM = 64
RW_LNX_EPS = 64e-5


LANES = 128
SUBLANES = 8
VMEM_LIMIT = 56 * 1024 * 1024

ML_CHUNK = 256
SSM_CHUNK = 128
SSM_TAIL = 2 * SUBLANES
RW_CHUNK = 64
RW_CHUNKS_PER_STEP = 2

PROJ_ROWS = 512
PROJ_COLS = 1024
MIX_FFN_ROWS = 1024
MIX_FFN_SUB = 512
FFN_CHUNK = 256


def _params(sem):
    return pltpu.CompilerParams(dimension_semantics=sem, vmem_limit_bytes=VMEM_LIMIT)


def _dot(a, b, dims=((1,), (0,))):
    return lax.dot_general(a.astype(BF16), b.astype(BF16), (dims, ((), ())),
                           preferred_element_type=F32)


def _split3(a):
    hi = a.astype(BF16)
    r1 = a - hi.astype(F32)
    mid = r1.astype(BF16)
    lo = (r1 - mid.astype(F32)).astype(BF16)
    return hi, mid, lo


def _dot_sel(a, sel, dims=((1,), (0,))):
    sel = sel.astype(BF16)
    return sum(lax.dot_general(p, sel, (dims, ((), ())), preferred_element_type=F32)
               for p in _split3(a))


def _sel_dot(sel, a, dims=((1,), (0,))):
    sel = sel.astype(BF16)
    return sum(lax.dot_general(sel, p, (dims, ((), ())), preferred_element_type=F32)
               for p in _split3(a))


def _softplus(x):
    return jnp.maximum(x, 0.0) + jnp.log1p(jnp.exp(-jnp.abs(x)))


def _sigmoid(x):
    return 0.5 * jnp.tanh(0.5 * x) + 0.5


def _silu(x):
    return x * _sigmoid(x)


def _layer_norm(x, g, b):
    mu = jnp.mean(x, axis=-1, keepdims=True)
    xc = x - mu
    var = jnp.mean(xc * xc, axis=-1, keepdims=True)
    return xc * lax.rsqrt(var + LN_EPS) * g + b


def _tri(n, strict=False, upper=False):
    r = lax.broadcasted_iota(jnp.int32, (n, n), 0)
    c = lax.broadcasted_iota(jnp.int32, (n, n), 1)
    if upper:
        r, c = c, r
    return (c < r) if strict else (c <= r)


def _resident(a):
    return pl.BlockSpec(a.shape, lambda *_: (0,) * a.ndim, pipeline_mode=pl.Buffered(1))


def _in_proj_kernel(x_ref, w_ref, ws_ref, o_ref, os_ref, *, tn):
    x = x_ref[...]
    xb = x.astype(BF16)
    cols = [slice(j * tn, (j + 1) * tn) for j in range(w_ref.shape[1] // tn)]
    acc = _dot(xb, w_ref[:, cols[0]])
    for j in range(len(cols)):
        nxt = _dot(xb, w_ref[:, cols[j + 1]]) if j + 1 < len(cols) else None
        o_ref[:, cols[j]] = acc.astype(o_ref.dtype)
        acc = nxt
    x_lo = (x - xb.astype(F32)).astype(BF16)
    both = _dot(xb, ws_ref[...])
    os_ref[...] = both[:, :LANES] + both[:, LANES:] + _dot(x_lo, ws_ref[:, :LANES])


def _in_proj(x, w, n_main, tm, tn, name):
    t, k = x.shape
    assert t % tm == 0 and n_main % tn == 0 and w.shape[1] - n_main <= LANES
    w_main = w[:, :n_main].astype(BF16)
    w_few = jnp.pad(w[:, n_main:], ((0, 0), (0, LANES - (w.shape[1] - n_main))))
    w_hi = w_few.astype(BF16)
    w_lo = (w_few - w_hi.astype(F32)).astype(BF16)
    w_split = jnp.concatenate([w_hi, w_lo], axis=1)
    return pl.pallas_call(
        functools.partial(_in_proj_kernel, tn=tn),
        grid=(t // tm,),
        in_specs=[pl.BlockSpec((tm, k), lambda i: (i, 0)), _resident(w_main), _resident(w_split)],
        out_specs=[pl.BlockSpec((tm, n_main), lambda i: (i, 0)),
                   pl.BlockSpec((tm, LANES), lambda i: (i, 0))],
        out_shape=[jax.ShapeDtypeStruct((t, n_main), BF16),
                   jax.ShapeDtypeStruct((t, LANES), F32)],
        compiler_params=_params(("parallel",)),
        name=name,
    )(x, w_main, w_split)


def _mix_ffn_kernel(y_ref, wm_ref, res_ref, g1_ref, b1_ref, wi_ref, wo_ref, g2_ref, b2_ref,
                    o_ref, *, tf, ts):
    f = wo_ref.shape[0]
    nf = f // tf
    n_sub = y_ref.shape[0] // ts
    rows = [slice(s * ts, (s + 1) * ts) for s in range(n_sub)]
    mix = lambda s: _dot(y_ref[rows[s], :], wm_ref[...])
    ln1 = lambda s, acc: _layer_norm(DEEPNORM_ALPHA * res_ref[rows[s], :] + acc,
                                     g1_ref[...], b1_ref[...])

    def finish(s, x, acc):
        o_ref[rows[s], :] = _layer_norm(DEEPNORM_ALPHA * x + acc, g2_ref[...], b2_ref[...])

    x = ln1(0, mix(0))
    pending = None
    for s in range(n_sub):
        xb = x.astype(BF16)
        gate_up = lambda j: (_dot(xb, wi_ref[:, j * tf:(j + 1) * tf]),
                             _dot(xb, wi_ref[:, f + j * tf:f + (j + 1) * tf]))
        acc = None
        nxt = gate_up(0)
        if pending is not None:
            finish(*pending)
        mixed = mix(s + 1) if s + 1 < n_sub else None
        x_next = None
        for j in range(nf):
            gate, up = nxt
            if j + 1 < nf:
                nxt = gate_up(j + 1)
            if j == 1 and mixed is not None:
                x_next = ln1(s + 1, mixed)
            part = _dot(_silu(gate) * up, wo_ref[j * tf:(j + 1) * tf, :])
            acc = part if acc is None else acc + part
        pending = (s, x, acc)
        x = x_next
    finish(*pending)


def _mix_ffn(y, w_mix, res, g1, b1, w_in, w_out, g2, b2, tm, tf, ts, name):
    t, k = y.shape
    d = w_mix.shape[1]
    f = w_out.shape[0]
    assert t % tm == 0 and tm % ts == 0 and f % tf == 0 and tf % LANES == 0 and f // tf >= 2
    vecs = [v.reshape(1, d) for v in (g1, b1, g2, b2)]
    return pl.pallas_call(
        functools.partial(_mix_ffn_kernel, tf=tf, ts=ts),
        grid=(t // tm,),
        in_specs=[pl.BlockSpec((tm, k), lambda i: (i, 0)), _resident(w_mix),
                  pl.BlockSpec((tm, d), lambda i: (i, 0)), _resident(vecs[0]), _resident(vecs[1]),
                  _resident(w_in), _resident(w_out), _resident(vecs[2]), _resident(vecs[3])],
        out_specs=pl.BlockSpec((tm, d), lambda i: (i, 0)),
        out_shape=jax.ShapeDtypeStruct((t, d), F32),
        compiler_params=_params(("parallel",)),
        name=name,
    )(y, w_mix, res, vecs[0], vecs[1], w_in, w_out, vecs[2], vecs[3])


def _rows_to_chunk_rows(a, bsz, nc, l):
    c = a.shape[1]
    return a.reshape(bsz, nc, l, c).transpose(0, 1, 3, 2)


def _mlstm_kernel(q_ref, k_ref, v_ref, o_ref, gc_ref, gr_ref, bc_ref, br_ref, nw_ref,
                  out_ref, c_ref, n_ref, m_ref, *, l):
    h_, dk, dv = ML_HEADS, ML_DQK, ML_DV

    cur = pl.program_id(1) % 2
    nxt = 1 - cur

    @pl.when(pl.program_id(1) == 0)
    def _():
        c_ref[0] = jnp.zeros(c_ref.shape[1:], F32)
        n_ref[0] = jnp.zeros(n_ref.shape[1:], F32)
        m_ref[0] = jnp.zeros(m_ref.shape[1:], F32)

    cap = lambda t: ML_GATE_CAP * jnp.tanh(t / ML_GATE_CAP)
    gc = cap(gc_ref[...] + bc_ref[...])
    gr = cap(gr_ref[...] + br_ref[...])
    i_col, lf_col = gc[:, :h_], -_softplus(-gc[:, h_:])
    i_row, lf_row = gr[:h_, :], -_softplus(-gr[h_:, :])
    causal = _tri(l)
    b_col = _sel_dot(causal, lf_col)
    b_row = _dot_sel(lf_row, _tri(l, upper=True))
    scale = dk ** -0.5

    heads = range(h_)
    qs = [q_ref[:, h * dk:(h + 1) * dk] for h in heads]
    ks = [k_ref[:, h * dk:(h + 1) * dk] for h in heads]
    vs = [v_ref[:, h * dv:(h + 1) * dv] for h in heads]
    s_qk = [_dot(qs[h], ks[h], ((1,), (1,))) for h in heads]
    c_prev = [c_ref[cur, h] for h in heads]
    q_c = [_dot(qs[h], c_prev[h]) for h in heads]
    m_prev = [m_ref[cur, h][:, :1] for h in heads]
    og = [nw_ref[:, h * dv:(h + 1) * dv] * _sigmoid(o_ref[:, h * dv:(h + 1) * dv].astype(F32))
          for h in heads]

    m_t, wts, sc = [], [], []
    for h in heads:
        bc, br, ir = b_col[:, h:h + 1], b_row[h:h + 1, :], i_row[h:h + 1, :]
        dmat = jnp.where(causal, bc - br + ir, -jnp.inf)
        inter = bc + m_prev[h]
        m_t.append(jnp.maximum(inter, jnp.max(dmat, axis=1, keepdims=True)))
        wts.append(jnp.exp(dmat - m_t[h]) * scale)
        sc.append(jnp.exp(inter - m_t[h]) * scale)

    qk = [s_qk[h] * wts[h] for h in heads]
    num = [_dot(qk[h], vs[h]) + sc[h] * q_c[h] for h in heads]

    kw, dec, m_new = [], [], []
    for h in heads:
        bc, br = b_col[:, h:h + 1], b_row[h:h + 1, :]
        b_last = bc[l - 1:l, :]
        g_row = b_last - br + i_row[h:h + 1, :]
        g_col = b_last - bc + i_col[:, h:h + 1]
        m_new.append(jnp.maximum(b_last + m_prev[h], jnp.max(g_row, axis=1, keepdims=True)))
        kw.append(ks[h].astype(F32) * jnp.exp(g_col - m_new[h]))
        dec.append(jnp.exp(b_last + m_prev[h] - m_new[h]))
    kv = [_dot(kw[h], vs[h], ((0,), (0,))) for h in heads]

    n_prev = [n_ref[cur, h] for h in heads]
    for h in heads:
        c_ref[nxt, h] = dec[h] * c_prev[h] + kv[h]
        n_ref[nxt, h] = dec[h] * n_prev[h] + jnp.sum(kw[h], axis=0, keepdims=True)
        m_ref[nxt, h] = jnp.broadcast_to(m_new[h], (1, LANES))
    qk_sum = [jnp.sum(qk[h], axis=1, keepdims=True) for h in heads]
    qn_sum = [jnp.sum(qs[h].astype(F32) * n_prev[h], axis=1, keepdims=True) for h in heads]
    hh = [num[h] / jnp.maximum(jnp.abs(qk_sum[h] + sc[h] * qn_sum[h]), jnp.exp(-m_t[h]))
          for h in heads]
    ms = [jnp.mean(hh[h] * hh[h], axis=1, keepdims=True) for h in heads]
    for h in heads:
        sl = slice(h * dv, (h + 1) * dv)
        hn = hh[h] * lax.rsqrt(ms[h] + RMS_EPS)
        out_ref[:, sl] = (hn * og[h]).astype(out_ref.dtype)


def _mlstm_core(proj, gpre, b_gate, norm_w, bsz, seq, l):
    h_, dk, dv = ML_HEADS, ML_DQK, ML_DV
    nc = seq // l
    t = bsz * seq
    row = lambda b, c: (b * nc + c, 0)
    g_rows = _rows_to_chunk_rows(gpre, bsz, nc, l)
    return pl.pallas_call(
        functools.partial(_mlstm_kernel, l=l),
        grid=(bsz, nc),
        in_specs=[pl.BlockSpec((l, h_ * dk), lambda b, c: (b * nc + c, 0)),
                  pl.BlockSpec((l, h_ * dk), lambda b, c: (b * nc + c, 1)),
                  pl.BlockSpec((l, h_ * dv), lambda b, c: (b * nc + c, 1)),
                  pl.BlockSpec((l, h_ * dv), lambda b, c: (b * nc + c, 2)),
                  pl.BlockSpec((l, 2 * h_), row),
                  pl.BlockSpec((None, None, 2 * h_, l), lambda b, c: (b, c, 0, 0)),
                  pl.BlockSpec((1, 2 * h_), lambda b, c: (0, 0)),
                  pl.BlockSpec((2 * h_, 1), lambda b, c: (0, 0)),
                  pl.BlockSpec((1, h_ * dv), lambda b, c: (0, 0))],
        out_specs=pl.BlockSpec((l, h_ * dv), row),
        out_shape=jax.ShapeDtypeStruct((t, h_ * dv), BF16),
        scratch_shapes=[pltpu.VMEM((2, h_, dk, dv), F32),
                        pltpu.VMEM((2, h_, 1, dk), F32),
                        pltpu.VMEM((2, h_, 1, LANES), F32)],
        compiler_params=_params(("parallel", "arbitrary")),
        name="mlstm_core",
    )(proj, proj, proj, proj, gpre, g_rows, b_gate.reshape(1, 2 * h_),
      b_gate.reshape(2 * h_, 1), norm_w.reshape(1, h_ * dv))


def _mlstm_mixer(x, w_in, b_gate, norm_w, bsz, seq):
    proj, gpre = _in_proj(x, w_in, ML_QKVO, PROJ_ROWS, PROJ_COLS, "mlstm_in")
    return _mlstm_core(proj, gpre[:, :2 * ML_HEADS], b_gate, norm_w, bsz, seq, ML_CHUNK)


def _ssd_kernel(z_ref, x_ref, bm_ref, cm_ref, dtc_ref, dtr_ref,
                cwx_ref, cwb_ref, cwc_ref, cbx_ref, cbb_ref, cbc_ref,
                dbc_ref, dbr_ref, alc_ref, alr_ref, dsk_ref, nw_ref,
                out_ref, st_ref, px_ref, pb_ref, pc_ref, *, l):
    g_, hpg, p, n, gw = SSM_GROUPS, SSM_HPG, SSM_HEADDIM, SSM_STATE, SSM_GW
    assert l == LANES and 2 * p == LANES
    tail = SSM_TAIL

    cur = pl.program_id(1) % 2
    nxt = 1 - cur

    @pl.when(pl.program_id(1) == 0)
    def _():
        st_ref[0] = jnp.zeros(st_ref.shape[1:], F32)
        px_ref[0] = jnp.zeros(px_ref.shape[1:], BF16)
        pb_ref[0] = jnp.zeros(pb_ref.shape[1:], BF16)
        pc_ref[0] = jnp.zeros(pc_ref.shape[1:], BF16)

    taps = SSM_CONV
    sel_r = lax.broadcasted_iota(jnp.int32, (taps * l, l + tail), 0)
    sel_c = lax.broadcasted_iota(jnp.int32, (taps * l, l + tail), 1)
    want_c = sel_r + (tail - (taps - 1))
    for j in range(1, taps):
        want_c = jnp.where(sel_r >= j * l, sel_r + (tail - (taps - 1) + j - j * l), want_c)
    shift_sel = jnp.where(sel_c == want_c, 1.0, 0.0).astype(BF16)

    def conv_silu(tail_ref, cur_ref, w_ref, b_ref):
        now = cur_ref[...]
        sh = lax.dot_general(shift_sel, jnp.concatenate([tail_ref[cur], now], axis=0),
                             (((1,), (0,)), ((), ())), preferred_element_type=F32)
        tail_ref[nxt] = now[l - tail:, :]
        acc = b_ref[...]
        for j in range(taps):
            acc = acc + w_ref[j:j + 1, :] * sh[j * l:(j + 1) * l]
        return _silu(acc)

    dt_c = _softplus(dtc_ref[...] + dbc_ref[...])
    dt_r = _softplus(dtr_ref[...] + dbr_ref[...])
    causal = _tri(l)
    cum_c = _sel_dot(causal, dt_c * -jnp.exp(alc_ref[...])) * LOG2E
    cum_r = _dot_sel(dt_r * -jnp.exp(alr_ref[...]), _tri(l, upper=True)) * LOG2E
    sub_r = cum_r - jnp.log(dt_r) * LOG2E
    lo = lax.broadcasted_iota(jnp.int32, (l, LANES), 1) < p
    splat = lambda a, c: jnp.broadcast_to(a[:, c:c + 1], (a.shape[0], LANES))

    b_all = conv_silu(pb_ref, bm_ref, cwb_ref, cbb_ref).astype(BF16)
    c_all = conv_silu(pc_ref, cm_ref, cwc_ref, cbc_ref).astype(BF16)
    x_all = conv_silu(px_ref, x_ref, cwx_ref, cbx_ref)

    for g in range(g_):
        ncols = slice(g * n, (g + 1) * n)
        bg, cg = b_all[:, ncols], c_all[:, ncols]
        cbm = jnp.where(causal, _dot(cg, bg, ((1,), (1,))), 0.0)
        st = st_ref[cur, g]
        y_inter = _dot(cg, st)

        xs, ys, xw, dec = [], [], [], []
        for pr in range(hpg // 2):
            he, ho = g * hpg + 2 * pr, g * hpg + 2 * pr + 1
            cols = slice(g * gw + pr * LANES, g * gw + (pr + 1) * LANES)
            x_pair = x_all[:, cols]
            cc_e, cc_o = splat(cum_c, he), splat(cum_c, ho)
            mm_e = cbm * jnp.exp2(jnp.minimum(cc_e - sub_r[he:he + 1, :], EXP2_CAP))
            mm_o = cbm * jnp.exp2(jnp.minimum(cc_o - sub_r[ho:ho + 1, :], EXP2_CAP))
            yy = _dot(jnp.concatenate([mm_e, mm_o], axis=0), x_pair)
            cc = jnp.where(lo, cc_e, cc_o)
            c_last = cc[l - 1:l, :]
            w_s = jnp.exp2(c_last - cc) * jnp.where(lo, splat(dt_c, he), splat(dt_c, ho))
            xs.append(x_pair)
            xw.append((x_pair * w_s).astype(BF16))
            dec.append(jnp.exp2(c_last))
            ys.append(jnp.where(lo, yy[:l], yy[l:])
                      + y_inter[:, pr * LANES:(pr + 1) * LANES] * jnp.exp2(cc))

        gcols = slice(g * gw, (g + 1) * gw)
        st_ref[nxt, g] = (st * jnp.concatenate(dec, axis=1)
                     + _dot(bg, jnp.concatenate(xw, axis=1), ((0,), (0,))))
        y = jnp.concatenate(ys, axis=1) + jnp.concatenate(xs, axis=1) * dsk_ref[:, gcols]
        y = y * _silu(z_ref[:, gcols].astype(F32))
        y = y * lax.rsqrt(jnp.mean(y * y, axis=1, keepdims=True) + RMS_EPS)
        out_ref[:, gcols] = (y * nw_ref[:, gcols]).astype(out_ref.dtype)


def _ssd_core(proj, dtpre, conv_w, conv_b, dt_bias, a_log, d_skip, norm_w, bsz, seq, l):
    g_, p, n, h_ = SSM_GROUPS, SSM_HEADDIM, SSM_STATE, SSM_HEADS
    di, gn = SSM_DINNER, SSM_GROUPS * SSM_STATE
    nc = seq // l
    t = bsz * seq
    dt_rows = _rows_to_chunk_rows(dtpre, bsz, nc, l)
    d_cols = jnp.repeat(d_skip, p).reshape(1, di)
    cwx, cwb, cwc = conv_w[:, :di], conv_w[:, di:di + gn], conv_w[:, di + gn:]
    cb2 = conv_b.reshape(1, -1)
    cbx, cbb, cbc = cb2[:, :di], cb2[:, di:di + gn], cb2[:, di + gn:]
    row = lambda b, c: b * nc + c
    full = lambda a: pl.BlockSpec(a.shape, lambda b, c: (0,) * a.ndim)
    consts = [cwx, cwb, cwc, cbx, cbb, cbc, dt_bias.reshape(1, h_), dt_bias.reshape(h_, 1),
              a_log.reshape(1, h_), a_log.reshape(h_, 1), d_cols, norm_w.reshape(1, di)]
    return pl.pallas_call(
        functools.partial(_ssd_kernel, l=l),
        grid=(bsz, nc),
        in_specs=[pl.BlockSpec((l, di), lambda b, c: (row(b, c), 0)),
                  pl.BlockSpec((l, di), lambda b, c: (row(b, c), 1)),
                  pl.BlockSpec((l, gn), lambda b, c: (row(b, c), 2 * di // gn)),
                  pl.BlockSpec((l, gn), lambda b, c: (row(b, c), 2 * di // gn + 1)),
                  pl.BlockSpec((l, h_), lambda b, c: (row(b, c), 0)),
                  pl.BlockSpec((None, None, h_, l), lambda b, c: (b, c, 0, 0))]
                 + [full(a) for a in consts],
        out_specs=pl.BlockSpec((l, di), lambda b, c: (row(b, c), 0)),
        out_shape=jax.ShapeDtypeStruct((t, di), BF16),
        scratch_shapes=[pltpu.VMEM((2, g_, n, SSM_GW), F32),
                        pltpu.VMEM((2, SSM_TAIL, di), BF16),
                        pltpu.VMEM((2, SSM_TAIL, gn), BF16),
                        pltpu.VMEM((2, SSM_TAIL, gn), BF16)],
        compiler_params=_params(("parallel", "arbitrary")),
        name="ssd_core",
    )(proj, proj, proj, proj, dtpre, dt_rows, *consts)


def _mamba_mixer(x, w_in, conv_w, conv_b, dt_bias, a_log, d_skip, norm_w, bsz, seq):
    proj, dtpre = _in_proj(x, w_in, SSM_ZX, PROJ_ROWS, PROJ_COLS, "ssm_in")
    return _ssd_core(proj, dtpre[:, :SSM_HEADS], conv_w, conv_b, dt_bias, a_log, d_skip,
                     norm_w, bsz, seq, SSM_CHUNK)


def _rwkv_prep_kernel(x_ref, xl_ref, mix_ref, wrkv_ref, w0_ref, w1_ref, w2_ref,
                      a0_ref, a1_ref, a2_ref, g1_ref, g2_ref,
                      r_ref, k_ref, v_ref, lw_ref, a_ref, g_ref, *, tiles_per_seq):
    x = x_ref[...]
    first = pl.program_id(0) % tiles_per_seq == 0
    before = jnp.where(first, 0.0, xl_ref[SUBLANES - 1:SUBLANES, :])
    row = lax.broadcasted_iota(jnp.int32, x.shape, 0)
    xx = jnp.where(row == 0, before, pltpu.roll(x, 1, axis=0)) - x
    mixed = lambda j: x + xx * mix_ref[j:j + 1, :]
    r_ref[...] = _dot(mixed(0), wrkv_ref[0]).astype(r_ref.dtype)
    k_ref[...] = _dot(mixed(2), wrkv_ref[1]).astype(k_ref.dtype)
    v_ref[...] = _dot(mixed(3), wrkv_ref[2]).astype(v_ref.dtype)
    w_raw = w0_ref[...] + _dot(jnp.tanh(_dot(mixed(1), w1_ref[...])), w2_ref[...])
    w = -_softplus(-w_raw) - 0.5
    lw_ref[...] = -jnp.exp(w)
    a_ref[...] = _sigmoid(a0_ref[...] + _dot(_dot(mixed(4), a1_ref[...]), a2_ref[...])
                          ).astype(a_ref.dtype)
    g_ref[...] = _dot(_sigmoid(_dot(mixed(5), g1_ref[...])), g2_ref[...]).astype(g_ref.dtype)


def _rwkv_prep(x, mix, w_rkv, w0, w1, w2, a0, a1, a2, g1, g2, seq, tm):
    t, d = x.shape
    assert seq % tm == 0 and tm % SUBLANES == 0
    rowblk = pl.BlockSpec((tm, d), lambda i: (i, 0))
    prev8 = pl.BlockSpec((SUBLANES, d), lambda i: (jnp.maximum(i * (tm // SUBLANES) - 1, 0), 0))
    ws = [mix, w_rkv.astype(BF16), w0.reshape(1, d), w1.astype(BF16), w2.astype(BF16),
          a0.reshape(1, d), a1.astype(BF16), a2.astype(BF16), g1.astype(BF16),
          g2.astype(BF16)]
    dts = [BF16, BF16, BF16, F32, BF16, BF16]
    return pl.pallas_call(
        functools.partial(_rwkv_prep_kernel, tiles_per_seq=seq // tm),
        grid=(t // tm,),
        in_specs=[rowblk, prev8] + [_resident(a) for a in ws],
        out_specs=[rowblk] * 6,
        out_shape=[jax.ShapeDtypeStruct((t, d), dt) for dt in dts],
        compiler_params=_params(("parallel",)),
        name="rwkv_prep",
    )(x, x, *ws)


def _rwkv_scan_kernel(r_ref, k_ref, v_ref, lw_ref, a_ref, g_ref,
                      kk_ref, ka_ref, rk_ref, lnw_ref, lnb_ref,
                      out_ref, s_ref, *, l):
    hd = RW_HEADDIM
    assert l == hd and 2 * hd == LANES
    n_pairs = r_ref.shape[1] // LANES

    cur = pl.program_id(1) % 2
    nxt = 1 - cur

    @pl.when(pl.program_id(1) == 0)
    def _():
        s_ref[0] = jnp.zeros(s_ref.shape[1:], F32)

    lane =lax.broadcasted_iota(jnp.int32, (l, LANES), 1)
    tok_t = lax.broadcasted_iota(jnp.int32, (l, LANES), 0)
    tok_s = lane & (hd - 1)
    lo = lane < hd
    strict2 = tok_s < tok_t
    incl2 = tok_s <= tok_t
    mlo = lambda x: jnp.where(lo, x, 0.0)
    mhi = lambda x: jnp.where(lo, 0.0, x)
    swap = lambda x: pltpu.roll(x, hd, axis=1)
    cat0 = lambda *xs: jnp.concatenate(xs, axis=0)
    nt = ((1,), (1,))
    tn = ((0,), (0,))

    def half_sums(x):
        s_lo = jnp.sum(mlo(x), axis=1, keepdims=True)
        s_hi = jnp.sum(mhi(x), axis=1, keepdims=True)
        return jnp.where(lo, s_lo, s_hi)

    steps = l.bit_length() - 1
    n_sub = r_ref.shape[0] // l

    st = [dict(p=p, rows=slice(c * l, (c + 1) * l), cols=slice(p * LANES, (p + 1) * LANES))
          for c in range(n_sub) for p in range(n_pairs)]
    for s in st:
        s["lw"] = lw_ref[s["rows"], s["cols"]]
    for s in st:
        gcum = s["lw"]
        for k in range(steps):
            gcum = gcum + jnp.where(tok_t >= 2 ** k, pltpu.roll(gcum, 2 ** k, axis=0), 0.0)
        s["gcum"] = gcum
    for s in st:
        rows, cols = s["rows"], s["cols"]
        gcum, lw = s["gcum"], s.pop("lw")
        a_sig = a_ref[rows, cols].astype(F32)
        k_in = k_ref[rows, cols].astype(F32)
        g_last = gcum[l - 1:l, :]
        p_inv = jnp.exp(-gcum)
        p_tail = jnp.exp(g_last - gcum)
        kmod = k_in * (1.0 + (a_sig - 1.0) * ka_ref[:, cols])
        kk = k_in * kk_ref[:, cols]
        kk = kk * lax.rsqrt(jnp.maximum(half_sums(kk * kk), 1e-24))
        b_vec = kk * a_sig
        a_dec = -kk * jnp.exp(gcum - lw)
        r_dec = r_ref[rows, cols].astype(F32) * jnp.exp(gcum)
        b_inv, k_inv = b_vec * p_inv, kmod * p_inv
        s.update(a_dec=a_dec, r_dec=r_dec, kmod=kmod, p_last=jnp.exp(g_last),
                 bk_tail=cat0(b_vec * p_tail, kmod * p_tail).astype(BF16),
                 bk_inv=cat0(mlo(b_inv), mlo(k_inv), mhi(b_inv), mhi(k_inv)).astype(BF16))
        del s["gcum"]
    for s in st:
        s["pm"] = _dot(cat0(s["a_dec"], s["r_dec"]), s.pop("bk_inv"), nt)
    for s in st:
        pm = s.pop("pm")
        s["a_e"] = jnp.where(strict2, pm[:l, :LANES], 0.0)
        s["a_o"] = jnp.where(strict2, pm[:l, LANES:], 0.0)
        s["m_eo"] = cat0(jnp.where(incl2, pm[l:, :LANES], 0.0),
                         jnp.where(incl2, pm[l:, LANES:], 0.0)).astype(BF16)
        s["v"] = v_ref[s["rows"], s["cols"]].astype(F32)
        s["v_sw"] = swap(s["v"])
    for s in st:
        s["akv"] = _dot(cat0(s["a_e"], s["a_o"]), cat0(jnp.zeros_like(s["v_sw"]), s["v_sw"]))
    for s in st:
        akv, a_dec = s.pop("akv"), s.pop("a_dec")
        s["xx"] = cat0(mlo(a_dec) + mhi(akv[:l]), mhi(a_dec) + mlo(akv[l:]))
        s["n_bd"] = cat0(mlo(s.pop("a_e")), swap(mlo(s.pop("a_o"))))

    for j in range(steps):
        for s in st:
            nb = s["n_bd"].astype(BF16)
            if j + 1 < steps:
                out = _dot(nb, jnp.concatenate([s["xx"].astype(BF16), nb], axis=1))
                s["xx"] = s["xx"] + out[:, :LANES]
                s["n_bd"] = out[:, LANES:]
            else:
                s["xx"] = s["xx"] + _dot(nb, s["xx"])

    state = [s_ref[cur, p] for p in range(n_pairs)]
    for c in range(n_sub):
        sub = st[c * n_pairs:(c + 1) * n_pairs]
        for s in sub:
            x_e, x_o = s["xx"][:l], s["xx"][l:]
            s_pair = state[s["p"]]
            s["tt"] = _dot(cat0(mlo(x_e) + mhi(x_o), s["r_dec"]),
                           cat0(mlo(s_pair), mhi(s_pair)), nt)
        for s in sub:
            tt = s.pop("tt")
            x_e, x_o = s["xx"][:l], s["xx"][l:]
            uu = tt[:l] + swap(mhi(x_e) + mlo(x_o))
            s["y0"] = tt[l:]
            s["uv"] = cat0(uu, s["v"]).astype(BF16)
        for s in sub:
            s["yc"] = _dot(s["m_eo"], s["uv"])
            s["ds"] = _dot(s["uv"], s["bk_tail"], tn)
        for s in sub:
            yc, ds = s.pop("yc"), s.pop("ds")
            s["y"] = s["y0"] + mlo(yc[:l]) + mhi(yc[l:])
            state[s["p"]] = state[s["p"]] * s["p_last"] + jnp.where(lo, ds[:l], ds[l:])
    for p in range(n_pairs):
        s_ref[nxt, p] = state[p]

    for s in st:
        rows, cols = s["rows"], s["cols"]
        y = s["y"]
        inv_hd = 1.0 / hd
        mu = half_sums(y) * inv_hd
        yc = y - mu
        var = half_sums(yc * yc) * inv_hd
        yn = yc * lax.rsqrt(var + RW_LNX_EPS) * lnw_ref[:, cols] + lnb_ref[:, cols]
        bonus = half_sums(r_ref[rows, cols].astype(F32) * s["kmod"] * rk_ref[:, cols])
        yn = yn + bonus * s["v"]
        out_ref[rows, cols] = (yn * g_ref[rows, cols].astype(F32)).astype(out_ref.dtype)


def _rwkv_scan(r, k, v, lw, a, g, k_k, k_a, r_k, lnx_w, lnx_b, bsz, seq, l, n_sub):
    t, d = r.shape
    assert seq % (l * n_sub) == 0
    nc = seq // (l * n_sub)
    blk = pl.BlockSpec((l * n_sub, d), lambda b, c: (b * nc + c, 0))
    vec = pl.BlockSpec((1, d), lambda b, c: (0, 0))
    vecs = [a_.reshape(1, d) for a_ in (k_k, k_a, r_k, lnx_w, lnx_b)]
    return pl.pallas_call(
        functools.partial(_rwkv_scan_kernel, l=l),
        grid=(bsz, nc),
        in_specs=[blk] * 6 + [vec] * 5,
        out_specs=blk,
        out_shape=jax.ShapeDtypeStruct((t, d), BF16),
        scratch_shapes=[pltpu.VMEM((2, d // LANES, RW_HEADDIM, LANES), F32)],
        compiler_params=_params(("parallel", "arbitrary")),
        name="rwkv_scan",
    )(r, k, v, lw, a, g, *vecs)


def _rwkv_mixer(x, mix, w_rkv, w0, w1, w2, a0, a1, a2, g1, g2, k_k, k_a, r_k,
                lnx_w, lnx_b, bsz, seq):
    r, k, v, lw, a, g = _rwkv_prep(x, mix, w_rkv, w0, w1, w2, a0, a1, a2, g1, g2, seq, PROJ_ROWS)
    return _rwkv_scan(r, k, v, lw, a, g, k_k, k_a, r_k, lnx_w, lnx_b, bsz, seq,
                      RW_CHUNK, RW_CHUNKS_PER_STEP)


def kernel(x, ln_g, ln_b, ffn_w_in, ffn_w_out, ml_w_in, ml_b_gate, ml_norm_w, ml_w_out, ssm_w_in, ssm_conv_w, ssm_conv_b, ssm_dt_bias, ssm_a_log, ssm_d, ssm_norm_w, ssm_w_out, rw_mix, rw_w_rkv, rw_w0, rw_w1, rw_w2, rw_a0, rw_a1, rw_a2, rw_g1, rw_g2, rw_k_k, rw_k_a, rw_r_k, rw_lnx_w, rw_lnx_b, rw_w_out):
    bsz, seq, d = x.shape
    h = x.reshape(bsz * seq, d)
    for i in range(DEPTH):
        kind, j = i % N_MIXERS, i // N_MIXERS
        if kind == 0:
            y = _mlstm_mixer(h, ml_w_in[j], ml_b_gate[j], ml_norm_w[j], bsz, seq)
            w_mix = ml_w_out[j]
        elif kind == 1:
            y = _mamba_mixer(h, ssm_w_in[j], ssm_conv_w[j], ssm_conv_b[j], ssm_dt_bias[j],
                             ssm_a_log[j], ssm_d[j], ssm_norm_w[j], bsz, seq)
            w_mix = ssm_w_out[j]
        else:
            y = _rwkv_mixer(h, rw_mix[j], rw_w_rkv[j], rw_w0[j], rw_w1[j], rw_w2[j],
                            rw_a0[j], rw_a1[j], rw_a2[j], rw_g1[j], rw_g2[j], rw_k_k[j],
                            rw_k_a[j], rw_r_k[j].reshape(-1), rw_lnx_w[j], rw_lnx_b[j], bsz, seq)
            w_mix = rw_w_out[j]
        h = _mix_ffn(y, w_mix.astype(BF16), h, ln_g[i, 0], ln_b[i, 0],
                     ffn_w_in[i].astype(BF16), ffn_w_out[i].astype(BF16),
                     ln_g[i, 1], ln_b[i, 1], MIX_FFN_ROWS, FFN_CHUNK, MIX_FFN_SUB, "mix_ffn")
    return h.reshape(bsz, seq, d)
```

```python
import functools
import math

import jax
import jax.numpy as jnp
from jax import lax
from jax.experimental import pallas as pl
from jax.experimental.pallas import tpu as pltpu

F32 = jnp.float32
BF16 = jnp.bfloat16

D_MODEL = 1024
DEPTH = 4
N_MIXERS = 3
DEEPNORM_ALPHA = (2 * DEPTH) ** 0.25
LOG2E = 1.0 / math.log(2.0)
EXP2_CAP = 126.0
LN_EPS = 1e-5
RMS_EPS = 1e-6

ML_HEADS = 4
ML_DV = D_MODEL // ML_HEADS
ML_DQK = ML_DV // 2
ML_GATE_CAP = 15.0
ML_QKVO = 2 * ML_HEADS * ML_DQK + 2 * ML_HEADS * ML_DV

SSM_DINNER = 2 * D_MODEL
SSM_HEADDIM = 64
SSM_HEADS = SSM_DINNER // SSM_HEADDIM
SSM_STATE = 128
SSM_GROUPS = 4
SSM_CONV = 4
SSM_HPG = SSM_HEADS // SSM_GROUPS
SSM_GW = SSM_DINNER // SSM_GROUPS
SSM_ZX = 2 * SSM_DINNER + 2 * SSM_GROUPS * SSM_STATE

RW_HEADDIM = 64
RW_LNX_EPS = 64e-5


LANES = 128
SUBLANES = 8
VMEM_LIMIT = 56 * 1024 * 1024

ML_CHUNK = 256
SSM_CHUNK = 128
SSM_TAIL = 2 * SUBLANES
RW_CHUNK = 64
RW_CHUNKS_PER_STEP = 2

PROJ_ROWS = 1024
PROJ_COLS = 1024
MIX_FFN_ROWS = 1024
MIX_FFN_SUB = 512
FFN_CHUNK = 256


def _params(sem):
    return pltpu.CompilerParams(dimension_semantics=sem, vmem_limit_bytes=VMEM_LIMIT)


def _dot(a, b, dims=((1,), (0,))):
    return lax.dot_general(a.astype(BF16), b.astype(BF16), (dims, ((), ())),
                           preferred_element_type=F32)


def _split3(a):
    hi = a.astype(BF16)
    r1 = a - hi.astype(F32)
    mid = r1.astype(BF16)
    lo = (r1 - mid.astype(F32)).astype(BF16)
    return hi, mid, lo


def _dot_sel(a, sel, dims=((1,), (0,))):
    sel = sel.astype(BF16)
    return sum(lax.dot_general(p, sel, (dims, ((), ())), preferred_element_type=F32)
               for p in _split3(a))


def _sel_dot(sel, a, dims=((1,), (0,))):
    sel = sel.astype(BF16)
    return sum(lax.dot_general(sel, p, (dims, ((), ())), preferred_element_type=F32)
               for p in _split3(a))


def _softplus(x):
    return jnp.maximum(x, 0.0) + jnp.log1p(jnp.exp(-jnp.abs(x)))


def _sigmoid(x):
    return 0.5 * jnp.tanh(0.5 * x) + 0.5


def _silu(x):
    return x * _sigmoid(x)


def _layer_norm(x, g, b):
    mu = jnp.mean(x, axis=-1, keepdims=True)
    xc = x - mu
    var = jnp.mean(xc * xc, axis=-1, keepdims=True)
    return xc * lax.rsqrt(var + LN_EPS) * g + b


def _tri(n, strict=False, upper=False):
    r = lax.broadcasted_iota(jnp.int32, (n, n), 0)
    c = lax.broadcasted_iota(jnp.int32, (n, n), 1)
    if upper:
        r, c = c, r
    return (c < r) if strict else (c <= r)


def _resident(a):
    return pl.BlockSpec(a.shape, lambda *_: (0,) * a.ndim, pipeline_mode=pl.Buffered(1))


def _in_proj_kernel(x_ref, w_ref, ws_ref, o_ref, os_ref, *, tn):
    x = x_ref[...]
    xb = x.astype(BF16)
    cols = [slice(j * tn, (j + 1) * tn) for j in range(w_ref.shape[1] // tn)]
    acc = _dot(xb, w_ref[:, cols[0]])
    for j in range(len(cols)):
        nxt = _dot(xb, w_ref[:, cols[j + 1]]) if j + 1 < len(cols) else None
        o_ref[:, cols[j]] = acc.astype(o_ref.dtype)
        acc = nxt
    x_lo = (x - xb.astype(F32)).astype(BF16)
    both = _dot(xb, ws_ref[...])
    os_ref[...] = both[:, :LANES] + both[:, LANES:] + _dot(x_lo, ws_ref[:, :LANES])


def _in_proj(x, w, n_main, tm, tn, name):
    t, k = x.shape
    assert t % tm == 0 and n_main % tn == 0 and w.shape[1] - n_main <= LANES
    w_main = w[:, :n_main].astype(BF16)
    w_few = jnp.pad(w[:, n_main:], ((0, 0), (0, LANES - (w.shape[1] - n_main))))
    w_hi = w_few.astype(BF16)
    w_lo = (w_few - w_hi.astype(F32)).astype(BF16)
    w_split = jnp.concatenate([w_hi, w_lo], axis=1)
    return pl.pallas_call(
        functools.partial(_in_proj_kernel, tn=tn),
        grid=(t // tm,),
        in_specs=[pl.BlockSpec((tm, k), lambda i: (i, 0)), _resident(w_main), _resident(w_split)],
        out_specs=[pl.BlockSpec((tm, n_main), lambda i: (i, 0)),
                   pl.BlockSpec((tm, LANES), lambda i: (i, 0))],
        out_shape=[jax.ShapeDtypeStruct((t, n_main), BF16),
                   jax.ShapeDtypeStruct((t, LANES), F32)],
        compiler_params=_params(("parallel",)),
        name=name,
    )(x, w_main, w_split)


def _mix_ffn_kernel(y_ref, wm_ref, res_ref, g1_ref, b1_ref, wi_ref, wo_ref, g2_ref, b2_ref,
                    o_ref, *, tf, ts):
    f = wo_ref.shape[0]
    nf = f // tf
    n_sub = y_ref.shape[0] // ts
    rows = [slice(s * ts, (s + 1) * ts) for s in range(n_sub)]
    mix = lambda s: _dot(y_ref[rows[s], :], wm_ref[...])
    ln1 = lambda s, acc: _layer_norm(DEEPNORM_ALPHA * res_ref[rows[s], :] + acc,
                                     g1_ref[...], b1_ref[...])

    def finish(s, x, acc):
        o_ref[rows[s], :] = _layer_norm(DEEPNORM_ALPHA * x + acc, g2_ref[...], b2_ref[...])

    x = ln1(0, mix(0))
    pending = None
    for s in range(n_sub):
        xb = x.astype(BF16)
        gate_up = lambda j: (_dot(xb, wi_ref[:, j * tf:(j + 1) * tf]),
                             _dot(xb, wi_ref[:, f + j * tf:f + (j + 1) * tf]))
        acc = None
        nxt = gate_up(0)
        if pending is not None:
            finish(*pending)
        mixed = mix(s + 1) if s + 1 < n_sub else None
        x_next = None
        for j in range(nf):
            gate, up = nxt
            if j + 1 < nf:
                nxt = gate_up(j + 1)
            if j == 1 and mixed is not None:
                x_next = ln1(s + 1, mixed)
            part = _dot(_silu(gate) * up, wo_ref[j * tf:(j + 1) * tf, :])
            acc = part if acc is None else acc + part
        pending = (s, x, acc)
        x = x_next
    finish(*pending)


def _mix_ffn(y, w_mix, res, g1, b1, w_in, w_out, g2, b2, tm, tf, ts, name):
    t, k = y.shape
    d = w_mix.shape[1]
    f = w_out.shape[0]
    assert t % tm == 0 and tm % ts == 0 and f % tf == 0 and tf % LANES == 0 and f // tf >= 2
    vecs = [v.reshape(1, d) for v in (g1, b1, g2, b2)]
    return pl.pallas_call(
        functools.partial(_mix_ffn_kernel, tf=tf, ts=ts),
        grid=(t // tm,),
        in_specs=[pl.BlockSpec((tm, k), lambda i: (i, 0)), _resident(w_mix),
                  pl.BlockSpec((tm, d), lambda i: (i, 0)), _resident(vecs[0]), _resident(vecs[1]),
                  _resident(w_in), _resident(w_out), _resident(vecs[2]), _resident(vecs[3])],
        out_specs=pl.BlockSpec((tm, d), lambda i: (i, 0)),
        out_shape=jax.ShapeDtypeStruct((t, d), F32),
        compiler_params=_params(("parallel",)),
        name=name,
    )(y, w_mix, res, vecs[0], vecs[1], w_in, w_out, vecs[2], vecs[3])


def _rows_to_chunk_rows(a, bsz, nc, l):
    c = a.shape[1]
    return a.reshape(bsz, nc, l, c).transpose(0, 1, 3, 2)


def _mlstm_kernel(q_ref, k_ref, v_ref, o_ref, gc_ref, gr_ref, bc_ref, br_ref, nw_ref,
                  out_ref, c_ref, n_ref, m_ref, *, l):
    h_, dk, dv = ML_HEADS, ML_DQK, ML_DV

    cur = pl.program_id(1) % 2
    nxt = 1 - cur

    @pl.when(pl.program_id(1) == 0)
    def _():
        c_ref[0] = jnp.zeros(c_ref.shape[1:], F32)
        n_ref[0] = jnp.zeros(n_ref.shape[1:], F32)
        m_ref[0] = jnp.zeros(m_ref.shape[1:], F32)

    cap = lambda t: ML_GATE_CAP * jnp.tanh(t / ML_GATE_CAP)
    gc = cap(gc_ref[...] + bc_ref[...])
    gr = cap(gr_ref[...] + br_ref[...])
    i_col, lf_col = gc[:, :h_], -_softplus(-gc[:, h_:])
    i_row, lf_row = gr[:h_, :], -_softplus(-gr[h_:, :])
    causal = _tri(l)
    b_col = _sel_dot(causal, lf_col)
    b_row = _dot_sel(lf_row, _tri(l, upper=True))
    scale = dk ** -0.5

    heads = range(h_)
    qs = [q_ref[:, h * dk:(h + 1) * dk] for h in heads]
    ks = [k_ref[:, h * dk:(h + 1) * dk] for h in heads]
    vs = [v_ref[:, h * dv:(h + 1) * dv] for h in heads]
    s_qk = [_dot(qs[h], ks[h], ((1,), (1,))) for h in heads]
    c_prev = [c_ref[cur, h] for h in heads]
    q_c = [_dot(qs[h], c_prev[h]) for h in heads]
    m_prev = [m_ref[cur, h][:, :1] for h in heads]
    og = [nw_ref[:, h * dv:(h + 1) * dv] * _sigmoid(o_ref[:, h * dv:(h + 1) * dv].astype(F32))
          for h in heads]

    m_t, wts, sc = [], [], []
    for h in heads:
        bc, br, ir = b_col[:, h:h + 1], b_row[h:h + 1, :], i_row[h:h + 1, :]
        dmat = jnp.where(causal, bc - br + ir, -jnp.inf)
        inter = bc + m_prev[h]
        m_t.append(jnp.maximum(inter, jnp.max(dmat, axis=1, keepdims=True)))
        wts.append(jnp.exp(dmat - m_t[h]) * scale)
        sc.append(jnp.exp(inter - m_t[h]) * scale)

    qk = [s_qk[h] * wts[h] for h in heads]
    num = [_dot(qk[h], vs[h]) + sc[h] * q_c[h] for h in heads]

    kw, dec, m_new = [], [], []
    for h in heads:
        bc, br = b_col[:, h:h + 1], b_row[h:h + 1, :]
        b_last = bc[l - 1:l, :]
        g_row = b_last - br + i_row[h:h + 1, :]
        g_col = b_last - bc + i_col[:, h:h + 1]
        m_new.append(jnp.maximum(b_last + m_prev[h], jnp.max(g_row, axis=1, keepdims=True)))
        kw.append(ks[h].astype(F32) * jnp.exp(g_col - m_new[h]))
        dec.append(jnp.exp(b_last + m_prev[h] - m_new[h]))
    kv = [_dot(kw[h], vs[h], ((0,), (0,))) for h in heads]

    n_prev = [n_ref[cur, h] for h in heads]
    for h in heads:
        c_ref[nxt, h] = dec[h] * c_prev[h] + kv[h]
        n_ref[nxt, h] = dec[h] * n_prev[h] + jnp.sum(kw[h], axis=0, keepdims=True)
        m_ref[nxt, h] = jnp.broadcast_to(m_new[h], (1, LANES))
    qk_sum = [jnp.sum(qk[h], axis=1, keepdims=True) for h in heads]
    qn_sum = [jnp.sum(qs[h].astype(F32) * n_prev[h], axis=1, keepdims=True) for h in heads]
    hh = [num[h] / jnp.maximum(jnp.abs(qk_sum[h] + sc[h] * qn_sum[h]), jnp.exp(-m_t[h]))
          for h in heads]
    ms = [jnp.mean(hh[h] * hh[h], axis=1, keepdims=True) for h in heads]
    for h in heads:
        sl = slice(h * dv, (h + 1) * dv)
        hn = hh[h] * lax.rsqrt(ms[h] + RMS_EPS)
        out_ref[:, sl] = (hn * og[h]).astype(out_ref.dtype)


def _mlstm_core(proj, gpre, b_gate, norm_w, bsz, seq, l):
    h_, dk, dv = ML_HEADS, ML_DQK, ML_DV
    nc = seq // l
    t = bsz * seq
    row = lambda b, c: (b * nc + c, 0)
    g_rows = _rows_to_chunk_rows(gpre, bsz, nc, l)
    return pl.pallas_call(
        functools.partial(_mlstm_kernel, l=l),
        grid=(bsz, nc),
        in_specs=[pl.BlockSpec((l, h_ * dk), lambda b, c: (b * nc + c, 0)),
                  pl.BlockSpec((l, h_ * dk), lambda b, c: (b * nc + c, 1)),
                  pl.BlockSpec((l, h_ * dv), lambda b, c: (b * nc + c, 1)),
                  pl.BlockSpec((l, h_ * dv), lambda b, c: (b * nc + c, 2)),
                  pl.BlockSpec((l, 2 * h_), row),
                  pl.BlockSpec((None, None, 2 * h_, l), lambda b, c: (b, c, 0, 0)),
                  pl.BlockSpec((1, 2 * h_), lambda b, c: (0, 0)),
                  pl.BlockSpec((2 * h_, 1), lambda b, c: (0, 0)),
                  pl.BlockSpec((1, h_ * dv), lambda b, c: (0, 0))],
        out_specs=pl.BlockSpec((l, h_ * dv), row),
        out_shape=jax.ShapeDtypeStruct((t, h_ * dv), BF16),
        scratch_shapes=[pltpu.VMEM((2, h_, dk, dv), F32),
                        pltpu.VMEM((2, h_, 1, dk), F32),
                        pltpu.VMEM((2, h_, 1, LANES), F32)],
        compiler_params=_params(("parallel", "arbitrary")),
        name="mlstm_core",
    )(proj, proj, proj, proj, gpre, g_rows, b_gate.reshape(1, 2 * h_),
      b_gate.reshape(2 * h_, 1), norm_w.reshape(1, h_ * dv))


def _mlstm_mixer(x, w_in, b_gate, norm_w, bsz, seq):
    proj, gpre = _in_proj(x, w_in, ML_QKVO, PROJ_ROWS, PROJ_COLS, "mlstm_in")
    return _mlstm_core(proj, gpre[:, :2 * ML_HEADS], b_gate, norm_w, bsz, seq, ML_CHUNK)


def _ssd_kernel(z_ref, x_ref, bm_ref, cm_ref, dtc_ref, dtr_ref,
                cwx_ref, cwb_ref, cwc_ref, cbx_ref, cbb_ref, cbc_ref,
                dbc_ref, dbr_ref, alc_ref, alr_ref, dsk_ref, nw_ref,
                out_ref, st_ref, px_ref, pb_ref, pc_ref, *, l):
    g_, hpg, p, n, gw = SSM_GROUPS, SSM_HPG, SSM_HEADDIM, SSM_STATE, SSM_GW
    assert l == LANES and 2 * p == LANES
    tail = SSM_TAIL

    cur = pl.program_id(1) % 2
    nxt = 1 - cur

    @pl.when(pl.program_id(1) == 0)
    def _():
        st_ref[0] = jnp.zeros(st_ref.shape[1:], F32)
        px_ref[0] = jnp.zeros(px_ref.shape[1:], BF16)
        pb_ref[0] = jnp.zeros(pb_ref.shape[1:], BF16)
        pc_ref[0] = jnp.zeros(pc_ref.shape[1:], BF16)

    taps = SSM_CONV
    sel_r = lax.broadcasted_iota(jnp.int32, (taps * l, l + tail), 0)
    sel_c = lax.broadcasted_iota(jnp.int32, (taps * l, l + tail), 1)
    want_c = sel_r + (tail - (taps - 1))
    for j in range(1, taps):
        want_c = jnp.where(sel_r >= j * l, sel_r + (tail - (taps - 1) + j - j * l), want_c)
    shift_sel = jnp.where(sel_c == want_c, 1.0, 0.0).astype(BF16)

    def conv_silu(tail_ref, cur_ref, w_ref, b_ref):
        now = cur_ref[...]
        sh = lax.dot_general(shift_sel, jnp.concatenate([tail_ref[cur], now], axis=0),
                             (((1,), (0,)), ((), ())), preferred_element_type=F32)
        tail_ref[nxt] = now[l - tail:, :]
        acc = b_ref[...]
        for j in range(taps):
            acc = acc + w_ref[j:j + 1, :] * sh[j * l:(j + 1) * l]
        return _silu(acc)

    dt_c = _softplus(dtc_ref[...] + dbc_ref[...])
    dt_r = _softplus(dtr_ref[...] + dbr_ref[...])
    causal = _tri(l)
    cum_c = _sel_dot(causal, dt_c * -jnp.exp(alc_ref[...])) * LOG2E
    cum_r = _dot_sel(dt_r * -jnp.exp(alr_ref[...]), _tri(l, upper=True)) * LOG2E
    sub_r = cum_r - jnp.log(dt_r) * LOG2E
    lo = lax.broadcasted_iota(jnp.int32, (l, LANES), 1) < p
    splat = lambda a, c: jnp.broadcast_to(a[:, c:c + 1], (a.shape[0], LANES))

    b_all = conv_silu(pb_ref, bm_ref, cwb_ref, cbb_ref).astype(BF16)
    c_all = conv_silu(pc_ref, cm_ref, cwc_ref, cbc_ref).astype(BF16)
    x_all = conv_silu(px_ref, x_ref, cwx_ref, cbx_ref)

    for g in range(g_):
        ncols = slice(g * n, (g + 1) * n)
        bg, cg = b_all[:, ncols], c_all[:, ncols]
        cbm = jnp.where(causal, _dot(cg, bg, ((1,), (1,))), 0.0)
        st = st_ref[cur, g]
        y_inter = _dot(cg, st)

        xs, ys, xw, dec = [], [], [], []
        for pr in range(hpg // 2):
            he, ho = g * hpg + 2 * pr, g * hpg + 2 * pr + 1
            cols = slice(g * gw + pr * LANES, g * gw + (pr + 1) * LANES)
            x_pair = x_all[:, cols]
            cc_e, cc_o = splat(cum_c, he), splat(cum_c, ho)
            mm_e = cbm * jnp.exp2(jnp.minimum(cc_e - sub_r[he:he + 1, :], EXP2_CAP))
            mm_o = cbm * jnp.exp2(jnp.minimum(cc_o - sub_r[ho:ho + 1, :], EXP2_CAP))
            yy = _dot(jnp.concatenate([mm_e, mm_o], axis=0), x_pair)
            cc = jnp.where(lo, cc_e, cc_o)
            c_last = cc[l - 1:l, :]
            w_s = jnp.exp2(c_last - cc) * jnp.where(lo, splat(dt_c, he), splat(dt_c, ho))
            xs.append(x_pair)
            xw.append((x_pair * w_s).astype(BF16))
            dec.append(jnp.exp2(c_last))
            ys.append(jnp.where(lo, yy[:l], yy[l:])
                      + y_inter[:, pr * LANES:(pr + 1) * LANES] * jnp.exp2(cc))

        gcols = slice(g * gw, (g + 1) * gw)
        st_ref[nxt, g] = (st * jnp.concatenate(dec, axis=1)
                     + _dot(bg, jnp.concatenate(xw, axis=1), ((0,), (0,))))
        y = jnp.concatenate(ys, axis=1) + jnp.concatenate(xs, axis=1) * dsk_ref[:, gcols]
        y = y * _silu(z_ref[:, gcols].astype(F32))
        y = y * lax.rsqrt(jnp.mean(y * y, axis=1, keepdims=True) + RMS_EPS)
        out_ref[:, gcols] = (y * nw_ref[:, gcols]).astype(out_ref.dtype)


def _ssd_core(proj, dtpre, conv_w, conv_b, dt_bias, a_log, d_skip, norm_w, bsz, seq, l):
    g_, p, n, h_ = SSM_GROUPS, SSM_HEADDIM, SSM_STATE, SSM_HEADS
    di, gn = SSM_DINNER, SSM_GROUPS * SSM_STATE
    nc = seq // l
    t = bsz * seq
    dt_rows = _rows_to_chunk_rows(dtpre, bsz, nc, l)
    d_cols = jnp.repeat(d_skip, p).reshape(1, di)
    cwx, cwb, cwc = conv_w[:, :di], conv_w[:, di:di + gn], conv_w[:, di + gn:]
    cb2 = conv_b.reshape(1, -1)
    cbx, cbb, cbc = cb2[:, :di], cb2[:, di:di + gn], cb2[:, di + gn:]
    row = lambda b, c: b * nc + c
    full = lambda a: pl.BlockSpec(a.shape, lambda b, c: (0,) * a.ndim)
    consts = [cwx, cwb, cwc, cbx, cbb, cbc, dt_bias.reshape(1, h_), dt_bias.reshape(h_, 1),
              a_log.reshape(1, h_), a_log.reshape(h_, 1), d_cols, norm_w.reshape(1, di)]
    return pl.pallas_call(
        functools.partial(_ssd_kernel, l=l),
        grid=(bsz, nc),
        in_specs=[pl.BlockSpec((l, di), lambda b, c: (row(b, c), 0)),
                  pl.BlockSpec((l, di), lambda b, c: (row(b, c), 1)),
                  pl.BlockSpec((l, gn), lambda b, c: (row(b, c), 2 * di // gn)),
                  pl.BlockSpec((l, gn), lambda b, c: (row(b, c), 2 * di // gn + 1)),
                  pl.BlockSpec((l, h_), lambda b, c: (row(b, c), 0)),
                  pl.BlockSpec((None, None, h_, l), lambda b, c: (b, c, 0, 0))]
                 + [full(a) for a in consts],
        out_specs=pl.BlockSpec((l, di), lambda b, c: (row(b, c), 0)),
        out_shape=jax.ShapeDtypeStruct((t, di), BF16),
        scratch_shapes=[pltpu.VMEM((2, g_, n, SSM_GW), F32),
                        pltpu.VMEM((2, SSM_TAIL, di), BF16),
                        pltpu.VMEM((2, SSM_TAIL, gn), BF16),
                        pltpu.VMEM((2, SSM_TAIL, gn), BF16)],
        compiler_params=_params(("parallel", "arbitrary")),
        name="ssd_core",
    )(proj, proj, proj, proj, dtpre, dt_rows, *consts)


def _mamba_mixer(x, w_in, conv_w, conv_b, dt_bias, a_log, d_skip, norm_w, bsz, seq):
    proj, dtpre = _in_proj(x, w_in, SSM_ZX, PROJ_ROWS, PROJ_COLS, "ssm_in")
    return _ssd_core(proj, dtpre[:, :SSM_HEADS], conv_w, conv_b, dt_bias, a_log, d_skip,
                     norm_w, bsz, seq, SSM_CHUNK)


def _rwkv_prep_kernel(x_ref, xl_ref, mix_ref, wrkv_ref, w0_ref, w1_ref, w2_ref,
                      a0_ref, a1_ref, a2_ref, g1_ref, g2_ref,
                      r_ref, k_ref, v_ref, lw_ref, a_ref, g_ref, *, tiles_per_seq):
    x = x_ref[...]
    first = pl.program_id(0) % tiles_per_seq == 0
    before = jnp.where(first, 0.0, xl_ref[SUBLANES - 1:SUBLANES, :])
    row = lax.broadcasted_iota(jnp.int32, x.shape, 0)
    xx = jnp.where(row == 0, before, pltpu.roll(x, 1, axis=0)) - x
    mixed = lambda j: x + xx * mix_ref[j:j + 1, :]
    r_ref[...] = _dot(mixed(0), wrkv_ref[0]).astype(r_ref.dtype)
    k_ref[...] = _dot(mixed(2), wrkv_ref[1]).astype(k_ref.dtype)
    v_ref[...] = _dot(mixed(3), wrkv_ref[2]).astype(v_ref.dtype)
    w_raw = w0_ref[...] + _dot(jnp.tanh(_dot(mixed(1), w1_ref[...])), w2_ref[...])
    w = -_softplus(-w_raw) - 0.5
    lw_ref[...] = -jnp.exp(w)
    a_ref[...] = _sigmoid(a0_ref[...] + _dot(_dot(mixed(4), a1_ref[...]), a2_ref[...])
                          ).astype(a_ref.dtype)
    g_ref[...] = _dot(_sigmoid(_dot(mixed(5), g1_ref[...])), g2_ref[...]).astype(g_ref.dtype)


def _rwkv_prep(x, mix, w_rkv, w0, w1, w2, a0, a1, a2, g1, g2, seq, tm):
    t, d = x.shape
    assert seq % tm == 0 and tm % SUBLANES == 0
    rowblk = pl.BlockSpec((tm, d), lambda i: (i, 0))
    prev8 = pl.BlockSpec((SUBLANES, d), lambda i: (jnp.maximum(i * (tm // SUBLANES) - 1, 0), 0))
    ws = [mix, w_rkv.astype(BF16), w0.reshape(1, d), w1.astype(BF16), w2.astype(BF16),
          a0.reshape(1, d), a1.astype(BF16), a2.astype(BF16), g1.astype(BF16),
          g2.astype(BF16)]
    dts = [BF16, BF16, BF16, F32, BF16, BF16]
    return pl.pallas_call(
        functools.partial(_rwkv_prep_kernel, tiles_per_seq=seq // tm),
        grid=(t // tm,),
        in_specs=[rowblk, prev8] + [_resident(a) for a in ws],
        out_specs=[rowblk] * 6,
        out_shape=[jax.ShapeDtypeStruct((t, d), dt) for dt in dts],
        compiler_params=_params(("parallel",)),
        name="rwkv_prep",
    )(x, x, *ws)


def _rwkv_scan_kernel(r_ref, k_ref, v_ref, lw_ref, a_ref, g_ref,
                      kk_ref, ka_ref, rk_ref, lnw_ref, lnb_ref,
                      out_ref, s_ref, *, l):
    hd = RW_HEADDIM
    assert l == hd and 2 * hd == LANES
    n_pairs = r_ref.shape[1] // LANES

    cur = pl.program_id(1) % 2
    nxt = 1 - cur

    @pl.when(pl.program_id(1) == 0)
    def _():
        s_ref[0] = jnp.zeros(s_ref.shape[1:], F32)

    lane =lax.broadcasted_iota(jnp.int32, (l, LANES), 1)
    tok_t = lax.broadcasted_iota(jnp.int32, (l, LANES), 0)
    tok_s = lane & (hd - 1)
    lo = lane < hd
    strict2 = tok_s < tok_t
    incl2 = tok_s <= tok_t
    mlo = lambda x: jnp.where(lo, x, 0.0)
    mhi = lambda x: jnp.where(lo, 0.0, x)
    swap = lambda x: pltpu.roll(x, hd, axis=1)
    cat0 = lambda *xs: jnp.concatenate(xs, axis=0)
    nt = ((1,), (1,))
    tn = ((0,), (0,))

    def half_sums(x):
        s_lo = jnp.sum(mlo(x), axis=1, keepdims=True)
        s_hi = jnp.sum(mhi(x), axis=1, keepdims=True)
        return jnp.where(lo, s_lo, s_hi)

    steps = l.bit_length() - 1
    n_sub = r_ref.shape[0] // l

    st = [dict(p=p, rows=slice(c * l, (c + 1) * l), cols=slice(p * LANES, (p + 1) * LANES))
          for c in range(n_sub) for p in range(n_pairs)]
    for s in st:
        s["lw"] = lw_ref[s["rows"], s["cols"]]
    for s in st:
        gcum = s["lw"]
        for k in range(steps):
            gcum = gcum + jnp.where(tok_t >= 2 ** k, pltpu.roll(gcum, 2 ** k, axis=0), 0.0)
        s["gcum"] = gcum
    for s in st:
        rows, cols = s["rows"], s["cols"]
        gcum, lw = s["gcum"], s.pop("lw")
        a_sig = a_ref[rows, cols].astype(F32)
        k_in = k_ref[rows, cols].astype(F32)
        g_last = gcum[l - 1:l, :]
        p_inv = jnp.exp(-gcum)
        p_tail = jnp.exp(g_last - gcum)
        kmod = k_in * (1.0 + (a_sig - 1.0) * ka_ref[:, cols])
        kk = k_in * kk_ref[:, cols]
        kk = kk * lax.rsqrt(jnp.maximum(half_sums(kk * kk), 1e-24))
        b_vec = kk * a_sig
        a_dec = -kk * jnp.exp(gcum - lw)
        r_dec = r_ref[rows, cols].astype(F32) * jnp.exp(gcum)
        b_inv, k_inv = b_vec * p_inv, kmod * p_inv
        s.update(a_dec=a_dec, r_dec=r_dec, kmod=kmod, p_last=jnp.exp(g_last),
                 bk_tail=cat0(b_vec * p_tail, kmod * p_tail).astype(BF16),
                 bk_inv=cat0(mlo(b_inv), mlo(k_inv), mhi(b_inv), mhi(k_inv)).astype(BF16))
        del s["gcum"]
    for s in st:
        s["pm"] = _dot(cat0(s["a_dec"], s["r_dec"]), s.pop("bk_inv"), nt)
    for s in st:
        pm = s.pop("pm")
        s["a_e"] = jnp.where(strict2, pm[:l, :LANES], 0.0)
        s["a_o"] = jnp.where(strict2, pm[:l, LANES:], 0.0)
        s["m_eo"] = cat0(jnp.where(incl2, pm[l:, :LANES], 0.0),
                         jnp.where(incl2, pm[l:, LANES:], 0.0)).astype(BF16)
        s["v"] = v_ref[s["rows"], s["cols"]].astype(F32)
        s["v_sw"] = swap(s["v"])
    for s in st:
        s["akv"] = _dot(cat0(s["a_e"], s["a_o"]), cat0(jnp.zeros_like(s["v_sw"]), s["v_sw"]))
    for s in st:
        akv, a_dec = s.pop("akv"), s.pop("a_dec")
        s["xx"] = cat0(mlo(a_dec) + mhi(akv[:l]), mhi(a_dec) + mlo(akv[l:]))
        s["n_bd"] = cat0(mlo(s.pop("a_e")), swap(mlo(s.pop("a_o"))))

    for j in range(steps):
        for s in st:
            nb = s["n_bd"].astype(BF16)
            if j + 1 < steps:
                out = _dot(nb, jnp.concatenate([s["xx"].astype(BF16), nb], axis=1))
                s["xx"] = s["xx"] + out[:, :LANES]
                s["n_bd"] = out[:, LANES:]
            else:
                s["xx"] = s["xx"] + _dot(nb, s["xx"])

    state = [s_ref[cur, p] for p in range(n_pairs)]
    for c in range(n_sub):
        sub = st[c * n_pairs:(c + 1) * n_pairs]
        for s in sub:
            x_e, x_o = s["xx"][:l], s["xx"][l:]
            s_pair = state[s["p"]]
            s["tt"] = _dot(cat0(mlo(x_e) + mhi(x_o), s["r_dec"]),
                           cat0(mlo(s_pair), mhi(s_pair)), nt)
        for s in sub:
            tt = s.pop("tt")
            x_e, x_o = s["xx"][:l], s["xx"][l:]
            uu = tt[:l] + swap(mhi(x_e) + mlo(x_o))
            s["y0"] = tt[l:]
            s["uv"] = cat0(uu, s["v"]).astype(BF16)
        for s in sub:
            s["yc"] = _dot(s["m_eo"], s["uv"])
            s["ds"] = _dot(s["uv"], s["bk_tail"], tn)
        for s in sub:
            yc, ds = s.pop("yc"), s.pop("ds")
            s["y"] = s["y0"] + mlo(yc[:l]) + mhi(yc[l:])
            state[s["p"]] = state[s["p"]] * s["p_last"] + jnp.where(lo, ds[:l], ds[l:])
    for p in range(n_pairs):
        s_ref[nxt, p] = state[p]

    for s in st:
        rows, cols = s["rows"], s["cols"]
        y = s["y"]
        inv_hd = 1.0 / hd
        mu = half_sums(y) * inv_hd
        yc = y - mu
        var = half_sums(yc * yc) * inv_hd
        yn = yc * lax.rsqrt(var + RW_LNX_EPS) * lnw_ref[:, cols] + lnb_ref[:, cols]
        bonus = half_sums(r_ref[rows, cols].astype(F32) * s["kmod"] * rk_ref[:, cols])
        yn = yn + bonus * s["v"]
        out_ref[rows, cols] = (yn * g_ref[rows, cols].astype(F32)).astype(out_ref.dtype)


def _rwkv_scan(r, k, v, lw, a, g, k_k, k_a, r_k, lnx_w, lnx_b, bsz, seq, l, n_sub):
    t, d = r.shape
    assert seq % (l * n_sub) == 0
    nc = seq // (l * n_sub)
    blk = pl.BlockSpec((l * n_sub, d), lambda b, c: (b * nc + c, 0))
    vec = pl.BlockSpec((1, d), lambda b, c: (0, 0))
    vecs = [a_.reshape(1, d) for a_ in (k_k, k_a, r_k, lnx_w, lnx_b)]
    return pl.pallas_call(
        functools.partial(_rwkv_scan_kernel, l=l),
        grid=(bsz, nc),
        in_specs=[blk] * 6 + [vec] * 5,
        out_specs=blk,
        out_shape=jax.ShapeDtypeStruct((t, d), BF16),
        scratch_shapes=[pltpu.VMEM((2, d // LANES, RW_HEADDIM, LANES), F32)],
        compiler_params=_params(("parallel", "arbitrary")),
        name="rwkv_scan",
    )(r, k, v, lw, a, g, *vecs)


def _rwkv_mixer(x, mix, w_rkv, w0, w1, w2, a0, a1, a2, g1, g2, k_k, k_a, r_k,
                lnx_w, lnx_b, bsz, seq):
    r, k, v, lw, a, g = _rwkv_prep(x, mix, w_rkv, w0, w1, w2, a0, a1, a2, g1, g2, seq, PROJ_ROWS)
    return _rwkv_scan(r, k, v, lw, a, g, k_k, k_a, r_k, lnx_w, lnx_b, bsz, seq,
                      RW_CHUNK, RW_CHUNKS_PER_STEP)


def kernel(x, ln_g, ln_b, ffn_w_in, ffn_w_out, ml_w_in, ml_b_gate, ml_norm_w, ml_w_out, ssm_w_in, ssm_conv_w, ssm_conv_b, ssm_dt_bias, ssm_a_log, ssm_d, ssm_norm_w, ssm_w_out, rw_mix, rw_w_rkv, rw_w0, rw_w1, rw_w2, rw_a0, rw_a1, rw_a2, rw_g1, rw_g2, rw_k_k, rw_k_a, rw_r_k, rw_lnx_w, rw_lnx_b, rw_w_out):
    bsz, seq, d = x.shape
    h = x.reshape(bsz * seq, d)
    for i in range(DEPTH):
        kind, j = i % N_MIXERS, i // N_MIXERS
        if kind == 0:
            y = _mlstm_mixer(h, ml_w_in[j], ml_b_gate[j], ml_norm_w[j], bsz, seq)
            w_mix = ml_w_out[j]
        elif kind == 1:
            y = _mamba_mixer(h, ssm_w_in[j], ssm_conv_w[j], ssm_conv_b[j], ssm_dt_bias[j],
                             ssm_a_log[j], ssm_d[j], ssm_norm_w[j], bsz, seq)
            w_mix = ssm_w_out[j]
        else:
            y = _rwkv_mixer(h, rw_mix[j], rw_w_rkv[j], rw_w0[j], rw_w1[j], rw_w2[j],
                            rw_a0[j], rw_a1[j], rw_a2[j], rw_g1[j], rw_g2[j], rw_k_k[j],
                            rw_k_a[j], rw_r_k[j].reshape(-1), rw_lnx_w[j], rw_lnx_b[j], bsz, seq)
            w_mix = rw_w_out[j]
        h = _mix_ffn(y, w_mix.astype(BF16), h, ln_g[i, 0], ln_b[i, 0],
                     ffn_w_in[i].astype(BF16), ffn_w_out[i].astype(BF16),
                     ln_g[i, 1], ln_b[i, 1], MIX_FFN_ROWS, FFN_CHUNK, MIX_FFN_SUB, "mix_ffn")
    return h.reshape(bsz, seq, d)
```
